```python
import math
import jax, jax.numpy as jnp
from jax import lax
import numpy as np

D_MODEL = 1024
BATCH = 2
SEQ = 16384
DEPTH = 1

A_HEADS = 8
A_KV_HEADS = 2
A_HEAD_DIM = 64
WINDOW = 128
BLK = WINDOW
A_WIDTH = A_HEADS * A_HEAD_DIM
NUM_BUCKETS = 32
T5_MAX_DIST = 128
B_HEADS = 4
Q_LORA = 256
KV_LORA = 128
NOPE_DIM = 128
ROPE_DIM = 64
V_DIM = 128
ROPE_THETA = 10000.0
QB = 128
B_WIDTH = B_HEADS * V_DIM
MIX_WIDTH = A_WIDTH + B_WIDTH
IN_SPLITS = (A_HEADS * A_HEAD_DIM, A_KV_HEADS * A_HEAD_DIM, A_KV_HEADS * A_HEAD_DIM,
             Q_LORA, KV_LORA, ROPE_DIM)
IN_COLS = sum(IN_SPLITS)
D_FF = 2816
CONV_W = 3
EPS = 1e-6
NEG = -1e30

kernel_name = "hymba_swa_sink_mla_convffn"


def rmsnorm(x, g):
    xf = x.astype(jnp.float32)
    y = xf * lax.rsqrt(jnp.mean(xf * xf, axis=-1, keepdims=True) + EPS)
    return (y * g.astype(jnp.float32)).astype(x.dtype)


def t5_bucket(dist):
    max_exact = NUM_BUCKETS // 2
    n = jnp.maximum(dist, 0)
    large = max_exact + (jnp.log(jnp.maximum(n, 1).astype(jnp.float32) / max_exact)
                         / math.log(T5_MAX_DIST / max_exact)
                         * (NUM_BUCKETS - max_exact)).astype(jnp.int32)
    large = jnp.minimum(large, NUM_BUCKETS - 1)
    return jnp.where(n < max_exact, n, large)


def rope(x, ang):
    half = x.shape[-1] // 2
    x1, x2 = x[..., :half].astype(jnp.float32), x[..., half:].astype(jnp.float32)
    c, s = jnp.cos(ang), jnp.sin(ang)
    return jnp.concatenate([x1 * c - x2 * s, x1 * s + x2 * c], axis=-1).astype(x.dtype)


def sliding_window_attention(q, k, v, sinks, bias_table):
    b, s, hq, dh = q.shape
    hkv = k.shape[2]
    g = hq // hkv
    nb = s // BLK
    qb = q.reshape(b, nb, BLK, hkv, g, dh)
    kb = k.reshape(b, nb, BLK, hkv, dh)
    vb = v.reshape(b, nb, BLK, hkv, dh)
    pad = jnp.zeros_like(kb[:, :1])
    kk = jnp.concatenate([jnp.concatenate([pad, kb[:, :-1]], axis=1), kb], axis=2)
    vv = jnp.concatenate([jnp.concatenate([pad, vb[:, :-1]], axis=1), vb], axis=2)
    scores = jnp.einsum('bnqhgd,bnkhd->bnhgqk', qb, kk).astype(jnp.float32) * (dh ** -0.5)
    q_idx = BLK + jnp.arange(BLK)
    k_idx = jnp.arange(2 * BLK)
    dist = q_idx[:, None] - k_idx[None, :]
    in_window = (dist >= 0) & (dist < WINDOW)
    not_pad = (jnp.arange(nb)[:, None, None] > 0) | (k_idx >= BLK)[None, None, :]
    mask = in_window[None] & not_pad
    bias = bias_table[t5_bucket(dist)].astype(jnp.float32)
    bias = jnp.transpose(bias, (2, 0, 1)).reshape(hkv, g, BLK, 2 * BLK)
    scores = jnp.where(mask[None, :, None, None], scores + bias, NEG)
    sink = sinks.astype(jnp.float32).reshape(1, 1, hkv, g, 1, 1)
    m = jnp.maximum(jnp.max(scores, axis=-1, keepdims=True), sink)
    p = jnp.exp(scores - m)
    denom = jnp.sum(p, axis=-1, keepdims=True) + jnp.exp(sink - m)
    out = jnp.einsum('bnhgqk,bnkhd->bnqhgd', (p / denom).astype(v.dtype), vv)
    return out.reshape(b, s, hq * dh)


def dense_causal_attention(q, k, v):
    b, s, h, dqk = q.shape
    dv = v.shape[-1]
    nb = s // QB
    scale = dqk ** -0.5
    k_pos = jnp.arange(s)

    def block(n):
        qs = lax.dynamic_slice_in_dim(q, n * QB, QB, axis=1)
        sc = jnp.einsum('bqhd,bkhd->bhqk', qs, k).astype(jnp.float32) * scale
        causal = (n * QB + jnp.arange(QB))[:, None] >= k_pos[None, :]
        p = jax.nn.softmax(jnp.where(causal, sc, NEG), axis=-1)
        return jnp.einsum('bhqk,bkhd->bqhd', p.astype(v.dtype), v)

    out = lax.map(block, jnp.arange(nb))
    return jnp.transpose(out, (1, 0, 2, 3, 4)).reshape(b, s, h * dv)


def causal_dwconv(u, w, bias):
    s = u.shape[1]
    up = jnp.pad(u, ((0, 0), (CONV_W - 1, 0), (0, 0)))
    return sum(up[:, j:j + s] * w[j] for j in range(CONV_W)) + bias


def setup_inputs(seed: int = 0) -> dict:
    key = jax.random.key(seed)
    ks = jax.random.split(key, 20)
    f32 = jnp.float32

    def w(k, shape, fan_in):
        return jax.random.normal(k, shape, f32) * fan_in ** -0.5

    def gain(k, shape):
        return 1.0 + 0.05 * jax.random.normal(k, shape, f32)

    x = jax.random.normal(ks[0], (BATCH, SEQ, D_MODEL), f32)
    offsets = jax.random.randint(ks[1], (BATCH, 1), 0, 4096, dtype=jnp.int32)
    positions = offsets + jnp.arange(SEQ, dtype=jnp.int32)[None, :]
    return {
        "x": x,
        "positions": positions,
        "rel_bias_table": 0.5 * jax.random.normal(ks[2], (NUM_BUCKETS, A_HEADS), f32),
        "attn_norm_g": gain(ks[3], (DEPTH, D_MODEL)),
        "w_in": w(ks[4], (DEPTH, D_MODEL, IN_COLS), D_MODEL),
        "sinks": 0.5 * jax.random.normal(ks[5], (DEPTH, A_HEADS), f32),
        "q_norm_g": gain(ks[6], (DEPTH, Q_LORA)),
        "w_q_b": w(ks[7], (DEPTH, Q_LORA, B_HEADS * (NOPE_DIM + ROPE_DIM)), Q_LORA),
        "kv_norm_g": gain(ks[8], (DEPTH, KV_LORA)),
        "w_kv_b": w(ks[9], (DEPTH, KV_LORA, B_HEADS * (NOPE_DIM + V_DIM)), KV_LORA),
        "a_out_norm_g": gain(ks[10], (DEPTH, A_WIDTH)),
        "b_out_norm_g": gain(ks[11], (DEPTH, B_WIDTH)),
        "w_out": w(ks[12], (DEPTH, MIX_WIDTH, D_MODEL), MIX_WIDTH),
        "ffn_norm_g": gain(ks[13], (DEPTH, D_MODEL)),
        "w_up": w(ks[14], (DEPTH, D_MODEL, 2 * D_FF), D_MODEL),
        "conv_w": w(ks[15], (DEPTH, CONV_W, 2 * D_FF), CONV_W),
        "conv_b": 0.01 * jax.random.normal(ks[16], (DEPTH, 2 * D_FF), f32),
        "w_down": w(ks[17], (DEPTH, D_FF, D_MODEL), D_FF),
        "final_norm_g": gain(ks[18], (D_MODEL,)),
    }


def reference(x, positions, rel_bias_table, attn_norm_g, w_in, sinks, q_norm_g, w_q_b,
              kv_norm_g, w_kv_b, a_out_norm_g, b_out_norm_g, w_out, ffn_norm_g, w_up,
              conv_w, conv_b, w_down, final_norm_g):
    b, s, _ = x.shape
    inv_freq = ROPE_THETA ** (-jnp.arange(0, ROPE_DIM, 2, dtype=jnp.float32) / ROPE_DIM)
    ang = positions.astype(jnp.float32)[..., None] * inv_freq
    cuts = np.cumsum(IN_SPLITS)[:-1].tolist()

    for l in range(DEPTH):
        h = rmsnorm(x, attn_norm_g[l])
        proj = h @ w_in[l]
        qa, ka, va, c_q, c_kv, k_pe = jnp.split(proj, cuts, axis=-1)

        qa = qa.reshape(b, s, A_HEADS, A_HEAD_DIM)
        ka = ka.reshape(b, s, A_KV_HEADS, A_HEAD_DIM)
        va = va.reshape(b, s, A_KV_HEADS, A_HEAD_DIM)
        out_a = sliding_window_attention(qa, ka, va, sinks[l], rel_bias_table)

        qb = (rmsnorm(c_q, q_norm_g[l]) @ w_q_b[l]).reshape(b, s, B_HEADS, NOPE_DIM + ROPE_DIM)
        q_nope, q_pe = qb[..., :NOPE_DIM], qb[..., NOPE_DIM:]
        q_pe = rope(q_pe, ang[:, :, None, :])
        kv = (rmsnorm(c_kv, kv_norm_g[l]) @ w_kv_b[l]).reshape(b, s, B_HEADS, NOPE_DIM + V_DIM)
        k_nope, vb = kv[..., :NOPE_DIM], kv[..., NOPE_DIM:]
        k_pe = jnp.broadcast_to(rope(k_pe, ang)[:, :, None, :], (b, s, B_HEADS, ROPE_DIM))
        qm = jnp.concatenate([q_nope, q_pe], axis=-1)
        km = jnp.concatenate([k_nope, k_pe], axis=-1)
        out_b = dense_causal_attention(qm, km, vb)

        mixed = jnp.concatenate([rmsnorm(out_a, a_out_norm_g[l]),
                                 rmsnorm(out_b, b_out_norm_g[l])], axis=-1)
        x = x + mixed @ w_out[l]

        h = rmsnorm(x, ffn_norm_g[l])
        u = causal_dwconv(h @ w_up[l], conv_w[l], conv_b[l])
        gate, val = u[..., :D_FF], u[..., D_FF:]
        x = x + (jax.nn.silu(gate) * val) @ w_down[l]

    return rmsnorm(x, final_norm_g)
```

```python
import functools
import math

import numpy as np
import jax
import jax.numpy as jnp
from jax import lax
from jax.experimental import pallas as pl
from jax.experimental.pallas import tpu as pltpu

F32 = jnp.float32
BF16 = jnp.bfloat16

D_MODEL = 1024
A_HEADS = 8
A_KV_HEADS = 2
A_HEAD_DIM = 64
A_GROUP = A_HEADS // A_KV_HEADS
WINDOW = 128
BLK = WINDOW
A_WIDTH = A_HEADS * A_HEAD_DIM
A_KV_WIDTH = A_KV_HEADS * A_HEAD_DIM
NUM_BUCKETS = 32
T5_MAX_DIST = 128
B_HEADS = 4
Q_LORA = 256
KV_LORA = 128
NOPE_DIM = 128
ROPE_DIM = 64
ROPE_HALF = ROPE_DIM // 2
V_DIM = 128
ROPE_THETA = 10000.0
B_WIDTH = B_HEADS * V_DIM
D_FF = 2816
CONV_W = 3
EPS = 1e-6
NEG = -1e30

LANES = 128
SUBLANES = 8
QK_PAD = 256
VMEM_LIMIT = 56 * 1024 * 1024

PROJ_TM = 512
SWA_ROWS = 512
MLA_TQ = 512
MLA_TK = 512
FFN_TM = 512
FFN_TF = 256

_C_QA = 0
_C_KA = _C_QA + A_WIDTH
_C_VA = _C_KA + A_KV_WIDTH
_C_CQ = _C_VA + A_KV_WIDTH
_C_CKV = _C_CQ + Q_LORA
_C_KPA = _C_CKV + KV_LORA
_C_KPB = _C_KPA + LANES
_C_END = _C_KPB + LANES
_QB_HEAD = 3 * LANES


def _rms(x):
    return x * lax.rsqrt(jnp.mean(x * x, axis=-1, keepdims=True) + EPS)


def _proj_kernel(x_ref, pos_ref, g_ref, w_in_ref, qg_ref, wq_ref, kvg_ref, wkv_ref,
                 freq_ref, sgn_ref, qa_ref, ka_ref, va_ref, qm_ref, km_ref, vb_ref):
    h = _rms(x_ref[...]) * g_ref[...]
    proj = jnp.dot(h.astype(BF16), w_in_ref[...], preferred_element_type=F32)
    qa_ref[...] = (proj[:, _C_QA:_C_KA] * (A_HEAD_DIM ** -0.5)).astype(BF16)
    ka_ref[...] = proj[:, _C_KA:_C_VA].astype(BF16)
    va_ref[...] = proj[:, _C_VA:_C_CQ].astype(BF16)

    ang = pos_ref[...].astype(F32) * freq_ref[...]
    cc = jnp.cos(ang)
    ss = jnp.sin(ang) * sgn_ref[...]
    kpe = (proj[:, _C_KPA:_C_KPB] * cc + proj[:, _C_KPB:_C_END] * ss).astype(BF16)

    qn = _rms(proj[:, _C_CQ:_C_CKV]) * qg_ref[...]
    qb = jnp.dot(qn.astype(BF16), wq_ref[...], preferred_element_type=F32)
    kvn = _rms(proj[:, _C_CKV:_C_KPA]) * kvg_ref[...]
    kv = jnp.dot(kvn.astype(BF16), wkv_ref[...], preferred_element_type=F32)

    scale = (NOPE_DIM + ROPE_DIM) ** -0.5
    for hd in range(B_HEADS):
        c0 = hd * _QB_HEAD
        nope = qb[:, c0:c0 + LANES]
        qa_part = qb[:, c0 + LANES:c0 + 2 * LANES]
        qb_part = qb[:, c0 + 2 * LANES:c0 + 3 * LANES]
        o0 = hd * QK_PAD
        qm_ref[:, o0:o0 + LANES] = (nope * scale).astype(BF16)
        qm_ref[:, o0 + LANES:o0 + 2 * LANES] = ((qa_part * cc + qb_part * ss) * scale).astype(BF16)
        km_ref[:, o0:o0 + LANES] = kv[:, hd * NOPE_DIM:(hd + 1) * NOPE_DIM].astype(BF16)
        km_ref[:, o0 + LANES:o0 + 2 * LANES] = kpe
    vb_ref[...] = kv[:, B_HEADS * NOPE_DIM:].astype(BF16)


def _proj_call(x2, pos2, g, w_in, qg, wq, kvg, wkv, freq, sgn):
    t = x2.shape[0]
    tm = PROJ_TM
    row = lambda i: (i, 0)
    fixed = lambda i: (0, 0)
    full = lambda a: pl.BlockSpec(a.shape, fixed)
    out_shape = (
        jax.ShapeDtypeStruct((t, A_WIDTH), BF16),
        jax.ShapeDtypeStruct((t, A_KV_WIDTH), BF16),
        jax.ShapeDtypeStruct((t, A_KV_WIDTH), BF16),
        jax.ShapeDtypeStruct((t, B_HEADS * QK_PAD), BF16),
        jax.ShapeDtypeStruct((t, B_HEADS * QK_PAD), BF16),
        jax.ShapeDtypeStruct((t, B_WIDTH), BF16),
    )
    return pl.pallas_call(
        _proj_kernel,
        out_shape=out_shape,
        grid=(t // tm,),
        in_specs=[pl.BlockSpec((tm, D_MODEL), row), pl.BlockSpec((tm, 1), row),
                  full(g), full(w_in), full(qg), full(wq), full(kvg), full(wkv),
                  full(freq), full(sgn)],
        out_specs=tuple(pl.BlockSpec((tm, s.shape[1]), row) for s in out_shape),
        compiler_params=pltpu.CompilerParams(
            dimension_semantics=("arbitrary",), vmem_limit_bytes=VMEM_LIMIT),
        name="proj",
    )(x2, pos2, g, w_in, qg, wq, kvg, wkv, freq, sgn)


def _swa_kernel(sinks_ref, tbl_ref, bucket_ref, q_ref, kp_ref, kc_ref, vp_ref, vc_ref,
                o_ref, bias_ref):
    step = pl.program_id(1)

    @pl.when((pl.program_id(0) == 0) & (step == 0))
    def _():
        bk = bucket_ref[...]
        for hd in range(A_HEADS):
            acc = jnp.full((BLK, 2 * BLK), NEG, F32)
            for b in range(NUM_BUCKETS):
                acc = jnp.where(bk == b, tbl_ref[b, hd], acc)
            bias_ref[hd] = acc

    kidx = lax.broadcasted_iota(jnp.int32, (BLK, 2 * BLK), 1)
    pad_mask = kidx < jnp.where(step == 0, BLK, 0)

    for r in range(SWA_ROWS // BLK):
        rows = slice(r * BLK, (r + 1) * BLK)
        if r == 0:
            k_prev, v_prev = kp_ref[...], vp_ref[...]
        else:
            prev = slice((r - 1) * BLK, r * BLK)
            k_prev, v_prev = kc_ref[prev, :], vc_ref[prev, :]
        kk = jnp.concatenate([k_prev, kc_ref[rows, :]], axis=0)
        vv = jnp.concatenate([v_prev, vc_ref[rows, :]], axis=0)
        q_blk = q_ref[rows, :]
        for hd in range(A_HEADS):
            g = hd // A_GROUP
            kcols = slice(g * A_HEAD_DIM, (g + 1) * A_HEAD_DIM)
            hcols = slice(hd * A_HEAD_DIM, (hd + 1) * A_HEAD_DIM)
            s = lax.dot_general(q_blk[:, hcols], kk[:, kcols], (((1,), (1,)), ((), ())),
                                preferred_element_type=F32)
            s = s + bias_ref[hd]
            if r == 0:
                s = jnp.where(pad_mask, NEG, s)
            sink = sinks_ref[hd]
            m = jnp.maximum(jnp.max(s, axis=-1, keepdims=True), sink)
            p = jnp.exp(s - m)
            denom = jnp.sum(p, axis=-1, keepdims=True) + jnp.exp(sink - m)
            pv = jnp.dot(p.astype(BF16), vv[:, kcols], preferred_element_type=F32)
            o_ref[rows, hcols] = pv / denom


def _swa_call(sinks, table, bucket, qa, ka, va):
    b, s, _ = qa.shape
    rows = SWA_ROWS
    per = rows // BLK
    cur = lambda bi, i: (bi, i, 0)
    prev = lambda bi, i: (bi, jnp.maximum(i * per - 1, 0), 0)
    smem = pl.BlockSpec(memory_space=pltpu.SMEM)
    return pl.pallas_call(
        _swa_kernel,
        out_shape=jax.ShapeDtypeStruct((b, s, A_WIDTH), F32),
        grid=(b, s // rows),
        in_specs=[smem, smem, pl.BlockSpec(bucket.shape, lambda bi, i: (0, 0)),
                  pl.BlockSpec((None, rows, A_WIDTH), cur),
                  pl.BlockSpec((None, BLK, A_KV_WIDTH), prev),
                  pl.BlockSpec((None, rows, A_KV_WIDTH), cur),
                  pl.BlockSpec((None, BLK, A_KV_WIDTH), prev),
                  pl.BlockSpec((None, rows, A_KV_WIDTH), cur)],
        out_specs=pl.BlockSpec((None, rows, A_WIDTH), cur),
        scratch_shapes=[pltpu.VMEM((A_HEADS, BLK, 2 * BLK), F32)],
        compiler_params=pltpu.CompilerParams(
            dimension_semantics=("arbitrary", "arbitrary"), vmem_limit_bytes=VMEM_LIMIT),
        name="swa",
    )(sinks, table, bucket, qa, ka, ka, va, va)


def _mla_kernel(q_ref, k_ref, v_ref, o_ref):
    qi = pl.program_id(2)
    q = q_ref[...]
    tq, tk = MLA_TQ, MLA_TK

    def chunk(j, carry, masked):
        m, l, acc = carry
        start = pl.multiple_of(j * tk, tk)
        k = k_ref[pl.ds(start, tk), :]
        v = v_ref[pl.ds(start, tk), :]
        s = lax.dot_general(q, k, (((1,), (1,)), ((), ())), preferred_element_type=F32)
        if masked:
            qpos = qi * tq + lax.broadcasted_iota(jnp.int32, (tq, tk), 0)
            kpos = j * tk + lax.broadcasted_iota(jnp.int32, (tq, tk), 1)
            s = jnp.where(qpos >= kpos, s, NEG)
        m_new = jnp.maximum(m, jnp.max(s, axis=-1, keepdims=True))
        alpha = jnp.exp(m - m_new)
        p = jnp.exp(s - m_new)
        l = alpha * l + jnp.sum(p, axis=-1, keepdims=True)
        acc = alpha * acc + jnp.dot(p.astype(BF16), v, preferred_element_type=F32)
        return m_new, l, acc

    init = (jnp.full((tq, 1), NEG, F32), jnp.zeros((tq, 1), F32), jnp.zeros((tq, V_DIM), F32))
    per = tq // tk
    carry = lax.fori_loop(0, qi * per, lambda j, c: chunk(j, c, False), init)
    for d in range(per):
        carry = chunk(qi * per + d, carry, True)
    _, l, acc = carry
    o_ref[...] = acc / l


def _mla_call(qm, km, vb):
    b, s, _ = qm.shape
    tq = MLA_TQ
    return pl.pallas_call(
        _mla_kernel,
        out_shape=jax.ShapeDtypeStruct((b, s, B_WIDTH), F32),
        grid=(b, B_HEADS, s // tq),
        in_specs=[pl.BlockSpec((None, tq, QK_PAD), lambda bi, h, i: (bi, i, h)),
                  pl.BlockSpec((None, s, QK_PAD), lambda bi, h, i: (bi, 0, h)),
                  pl.BlockSpec((None, s, V_DIM), lambda bi, h, i: (bi, 0, h))],
        out_specs=pl.BlockSpec((None, tq, V_DIM), lambda bi, h, i: (bi, i, h)),
        compiler_params=pltpu.CompilerParams(
            dimension_semantics=("arbitrary", "arbitrary", "arbitrary"),
            vmem_limit_bytes=VMEM_LIMIT),
        name="mla",
    )(qm, km, vb)


def _ffn_kernel(tiles_per_seq, x_ref, oa_ref, ob_ref, ag_ref, bg_ref, wout_ref, fg_ref,
                wup_ref, cw_ref, cb_ref, wdn_ref, ng_ref, o_ref,
                carry_ref, ug_ref, uv_ref, act_ref):
    tm, tf = FFN_TM, FFN_TF
    @pl.when((pl.program_id(0) % tiles_per_seq) == 0)
    def _():
        carry_ref[...] = jnp.zeros_like(carry_ref)

    mixed = jnp.concatenate([_rms(oa_ref[...]) * ag_ref[...],
                             _rms(ob_ref[...]) * bg_ref[...]], axis=-1)
    x1 = x_ref[...] + jnp.dot(mixed.astype(BF16), wout_ref[...], preferred_element_type=F32)
    h2 = (_rms(x1) * fg_ref[...]).astype(BF16)

    def conv(u_ref, u, cols):
        u_ref[0:SUBLANES, :] = carry_ref[:, cols]
        u_ref[SUBLANES:SUBLANES + tm, :] = u
        carry_ref[:, cols] = u_ref[tm:tm + SUBLANES, :]
        w = cw_ref[:, cols]
        y = (u_ref[SUBLANES - 2:SUBLANES - 2 + tm, :] * w[0:1]
             + u_ref[SUBLANES - 1:SUBLANES - 1 + tm, :] * w[1:2]
             + u * w[2:3])
        return y + cb_ref[:, cols]

    for j in range(D_FF // tf):
        gcols = slice(j * tf, (j + 1) * tf)
        vcols = slice(D_FF + j * tf, D_FF + (j + 1) * tf)
        ug = jnp.dot(h2, wup_ref[:, gcols], preferred_element_type=F32)
        uv = jnp.dot(h2, wup_ref[:, vcols], preferred_element_type=F32)
        gate = conv(ug_ref, ug, gcols)
        val = conv(uv_ref, uv, vcols)
        act_ref[:, gcols] = (gate * (1.0 / (1.0 + jnp.exp(-gate))) * val).astype(BF16)

    x2 = x1 + jnp.dot(act_ref[...], wdn_ref[...], preferred_element_type=F32)
    o_ref[...] = _rms(x2) * ng_ref[...]


def _ffn_call(x2, oa, ob, ag, bg, wout, fg, wup, cw, cb, wdn, ng, tiles_per_seq):
    t = x2.shape[0]
    tm = FFN_TM
    row = lambda i: (i, 0)
    fixed = lambda i: (0, 0)
    full = lambda a: pl.BlockSpec(a.shape, fixed, pipeline_mode=pl.Buffered(1))
    return pl.pallas_call(
        functools.partial(_ffn_kernel, tiles_per_seq),
        out_shape=jax.ShapeDtypeStruct((t, D_MODEL), F32),
        grid=(t // tm,),
        in_specs=[pl.BlockSpec((tm, D_MODEL), row), pl.BlockSpec((tm, A_WIDTH), row),
                  pl.BlockSpec((tm, B_WIDTH), row), full(ag), full(bg), full(wout), full(fg),
                  full(wup), full(cw), full(cb), full(wdn), full(ng)],
        out_specs=pl.BlockSpec((tm, D_MODEL), row),
        scratch_shapes=[pltpu.VMEM((SUBLANES, 2 * D_FF), F32),
                        pltpu.VMEM((tm + SUBLANES, FFN_TF), F32),
                        pltpu.VMEM((tm + SUBLANES, FFN_TF), F32),
                        pltpu.VMEM((tm, D_FF), BF16)],
        compiler_params=pltpu.CompilerParams(
            dimension_semantics=("arbitrary",), vmem_limit_bytes=VMEM_LIMIT),
        name="ffn",
    )(x2, oa, ob, ag, bg, wout, fg, wup, cw, cb, wdn, ng)


def _t5_bucket_matrix():
    q_idx = BLK + np.arange(BLK)
    k_idx = np.arange(2 * BLK)
    dist = q_idx[:, None] - k_idx[None, :]
    max_exact = NUM_BUCKETS // 2
    n = np.maximum(dist, 0)
    large = max_exact + (np.log(np.maximum(n, 1).astype(np.float32) / max_exact)
                         / math.log(T5_MAX_DIST / max_exact)
                         * (NUM_BUCKETS - max_exact)).astype(np.int32)
    large = np.minimum(large, NUM_BUCKETS - 1)
    bucket = np.where(n < max_exact, n, large)
    in_window = (dist >= 0) & (dist < WINDOW)
    return np.where(in_window, bucket, -1).astype(np.int32)


def _row(v):
    return v.reshape(1, -1).astype(F32)


def kernel(x, positions, rel_bias_table, attn_norm_g, w_in, sinks, q_norm_g, w_q_b, kv_norm_g,
           w_kv_b, a_out_norm_g, b_out_norm_g, w_out, ffn_norm_g, w_up, conv_w, conv_b, w_down,
           final_norm_g):
    b, s, d = x.shape
    t = b * s
    assert d == D_MODEL and s % max(PROJ_TM, SWA_ROWS, MLA_TQ, FFN_TM) == 0
    assert attn_norm_g.shape[0] == 1, "one trunk layer"
    l = 0

    zeros64 = jnp.zeros((D_MODEL, LANES - ROPE_DIM), F32)
    kp = w_in[l][:, _C_KPA:_C_KPA + ROPE_DIM]
    k1, k2 = kp[:, :ROPE_HALF], kp[:, ROPE_HALF:]
    w_in_x = jnp.concatenate([w_in[l][:, :_C_KPA], k1, k2, zeros64, k2, k1, zeros64],
                             axis=1).astype(BF16)
    wq = w_q_b[l].reshape(Q_LORA, B_HEADS, NOPE_DIM + ROPE_DIM)
    q1 = wq[:, :, NOPE_DIM:NOPE_DIM + ROPE_HALF]
    q2 = wq[:, :, NOPE_DIM + ROPE_HALF:]
    zq = jnp.zeros((Q_LORA, B_HEADS, LANES - ROPE_DIM), F32)
    wq_x = jnp.concatenate([wq[:, :, :NOPE_DIM], q1, q2, zq, q2, q1, zq], axis=2)
    wq_x = wq_x.reshape(Q_LORA, B_HEADS * _QB_HEAD).astype(BF16)
    wkv = w_kv_b[l].reshape(KV_LORA, B_HEADS, NOPE_DIM + V_DIM)
    wkv_x = jnp.concatenate([wkv[:, :, :NOPE_DIM].reshape(KV_LORA, -1),
                             wkv[:, :, NOPE_DIM:].reshape(KV_LORA, -1)], axis=1).astype(BF16)

    inv_freq = ROPE_THETA ** (-jnp.arange(0, ROPE_DIM, 2, dtype=F32) / ROPE_DIM)
    pad = jnp.zeros((LANES - ROPE_DIM,), F32)
    freq = jnp.concatenate([inv_freq, inv_freq, pad]).reshape(1, LANES)
    ones = jnp.ones((ROPE_HALF,), F32)
    sgn = jnp.concatenate([-ones, ones, pad]).reshape(1, LANES)

    x2 = x.reshape(t, d)
    pos2 = positions.reshape(t, 1).astype(jnp.int32)

    qa, ka, va, qm, km, vb = _proj_call(
        x2, pos2, _row(attn_norm_g[l]), w_in_x, _row(q_norm_g[l]), wq_x,
        _row(kv_norm_g[l]), wkv_x, freq, sgn)

    bucket = jnp.asarray(_t5_bucket_matrix())
    out_a = _swa_call(sinks[l].astype(F32), rel_bias_table.astype(F32), bucket,
                      qa.reshape(b, s, -1), ka.reshape(b, s, -1), va.reshape(b, s, -1))
    out_b = _mla_call(qm.reshape(b, s, -1), km.reshape(b, s, -1), vb.reshape(b, s, -1))

    cw = jnp.concatenate([conv_w[l], jnp.zeros((SUBLANES - CONV_W, 2 * D_FF), F32)], axis=0)
    out = _ffn_call(
        x2, out_a.reshape(t, -1), out_b.reshape(t, -1), _row(a_out_norm_g[l]),
        _row(b_out_norm_g[l]), w_out[l].astype(BF16), _row(ffn_norm_g[l]),
        w_up[l].astype(BF16), cw.astype(F32), _row(conv_b[l]), w_down[l].astype(BF16),
        _row(final_norm_g), s // FFN_TM)
    return out.reshape(b, s, d)
```

```python
import functools
import math

import numpy as np
import jax
import jax.numpy as jnp
from jax import lax
from jax.experimental import pallas as pl
from jax.experimental.pallas import tpu as pltpu

F32 = jnp.float32
BF16 = jnp.bfloat16

D_MODEL = 1024
A_HEADS = 8
A_KV_HEADS = 2
A_HEAD_DIM = 64
A_GROUP = A_HEADS // A_KV_HEADS
WINDOW = 128
BLK = WINDOW
A_WIDTH = A_HEADS * A_HEAD_DIM
A_KV_WIDTH = A_KV_HEADS * A_HEAD_DIM
NUM_BUCKETS = 32
T5_MAX_DIST = 128
B_HEADS = 4
Q_LORA = 256
KV_LORA = 128
NOPE_DIM = 128
ROPE_DIM = 64
ROPE_HALF = ROPE_DIM // 2
V_DIM = 128
ROPE_THETA = 10000.0
B_WIDTH = B_HEADS * V_DIM
D_FF = 2816
CONV_W = 3
EPS = 1e-6
NEG = -1e30

LANES = 128
SUBLANES = 8
QK_PAD = 256
VMEM_LIMIT = 56 * 1024 * 1024

PROJ_TM = 512
SWA_ROWS = 512
MLA_TQ = 1024
MLA_TK = 1024
MLA_TS = 512
FFN_TM = 512
FFN_TF = 256

_C_QA = 0
_C_KA = _C_QA + A_WIDTH
_C_VA = _C_KA + A_KV_WIDTH
_C_CQ = _C_VA + A_KV_WIDTH
_C_CKV = _C_CQ + Q_LORA
_C_KPA = _C_CKV + KV_LORA
_C_KPB = _C_KPA + LANES
_C_END = _C_KPB + LANES
_QB_HEAD = 3 * LANES


def _rms(x):
    return x * lax.rsqrt(jnp.mean(x * x, axis=-1, keepdims=True) + EPS)


def _proj_kernel(x_ref, pos_ref, g_ref, w_in_ref, qg_ref, wq_ref, kvg_ref, wkv_ref,
                 freq_ref, sgn_ref, qa_ref, ka_ref, va_ref, qm_ref, km_ref, vb_ref):
    h = _rms(x_ref[...]) * g_ref[...]
    proj = jnp.dot(h.astype(BF16), w_in_ref[...], preferred_element_type=F32)
    qa_ref[...] = (proj[:, _C_QA:_C_KA] * (A_HEAD_DIM ** -0.5)).astype(BF16)
    ka_ref[...] = proj[:, _C_KA:_C_VA].astype(BF16)
    va_ref[...] = proj[:, _C_VA:_C_CQ].astype(BF16)

    ang = pos_ref[...].astype(F32) * freq_ref[...]
    cc = jnp.cos(ang)
    ss = jnp.sin(ang) * sgn_ref[...]
    kpe = (proj[:, _C_KPA:_C_KPB] * cc + proj[:, _C_KPB:_C_END] * ss).astype(BF16)

    qn = _rms(proj[:, _C_CQ:_C_CKV]) * qg_ref[...]
    qb = jnp.dot(qn.astype(BF16), wq_ref[...], preferred_element_type=F32)
    kvn = _rms(proj[:, _C_CKV:_C_KPA]) * kvg_ref[...]
    kv = jnp.dot(kvn.astype(BF16), wkv_ref[...], preferred_element_type=F32)

    scale = (NOPE_DIM + ROPE_DIM) ** -0.5 * math.log2(math.e)
    for hd in range(B_HEADS):
        c0 = hd * _QB_HEAD
        nope = qb[:, c0:c0 + LANES]
        qa_part = qb[:, c0 + LANES:c0 + 2 * LANES]
        qb_part = qb[:, c0 + 2 * LANES:c0 + 3 * LANES]
        o0 = hd * QK_PAD
        qm_ref[:, o0:o0 + LANES] = (nope * scale).astype(BF16)
        qm_ref[:, o0 + LANES:o0 + 2 * LANES] = ((qa_part * cc + qb_part * ss) * scale).astype(BF16)
        km_ref[:, o0:o0 + LANES] = kv[:, hd * NOPE_DIM:(hd + 1) * NOPE_DIM].astype(BF16)
        km_ref[:, o0 + LANES:o0 + 2 * LANES] = kpe
    vb_ref[...] = kv[:, B_HEADS * NOPE_DIM:].astype(BF16)


def _proj_call(x2, pos2, g, w_in, qg, wq, kvg, wkv, freq, sgn):
    t = x2.shape[0]
    tm = PROJ_TM
    row = lambda i: (i, 0)
    fixed = lambda i: (0, 0)
    full = lambda a: pl.BlockSpec(a.shape, fixed)
    out_shape = (
        jax.ShapeDtypeStruct((t, A_WIDTH), BF16),
        jax.ShapeDtypeStruct((t, A_KV_WIDTH), BF16),
        jax.ShapeDtypeStruct((t, A_KV_WIDTH), BF16),
        jax.ShapeDtypeStruct((t, B_HEADS * QK_PAD), BF16),
        jax.ShapeDtypeStruct((t, B_HEADS * QK_PAD), BF16),
        jax.ShapeDtypeStruct((t, B_WIDTH), BF16),
    )
    return pl.pallas_call(
        _proj_kernel,
        out_shape=out_shape,
        grid=(t // tm,),
        in_specs=[pl.BlockSpec((tm, D_MODEL), row), pl.BlockSpec((tm, 1), row),
                  full(g), full(w_in), full(qg), full(wq), full(kvg), full(wkv),
                  full(freq), full(sgn)],
        out_specs=tuple(pl.BlockSpec((tm, s.shape[1]), row) for s in out_shape),
        compiler_params=pltpu.CompilerParams(
            dimension_semantics=("arbitrary",), vmem_limit_bytes=VMEM_LIMIT),
        name="proj",
    )(x2, pos2, g, w_in, qg, wq, kvg, wkv, freq, sgn)


def _swa_kernel(sinks_ref, tbl_ref, bucket_ref, q_ref, kp_ref, kc_ref, vp_ref, vc_ref,
                o_ref, bias_ref):
    step = pl.program_id(1)

    @pl.when((pl.program_id(0) == 0) & (step == 0))
    def _():
        bk = bucket_ref[...]
        for hd in range(A_HEADS):
            acc = jnp.full((BLK, 2 * BLK), NEG, F32)
            for b in range(NUM_BUCKETS):
                acc = jnp.where(bk == b, tbl_ref[b, hd], acc)
            bias_ref[hd] = acc

    kidx = lax.broadcasted_iota(jnp.int32, (BLK, 2 * BLK), 1)
    pad_mask = kidx < jnp.where(step == 0, BLK, 0)

    for r in range(SWA_ROWS // BLK):
        rows = slice(r * BLK, (r + 1) * BLK)
        if r == 0:
            k_prev, v_prev = kp_ref[...], vp_ref[...]
        else:
            prev = slice((r - 1) * BLK, r * BLK)
            k_prev, v_prev = kc_ref[prev, :], vc_ref[prev, :]
        kk = jnp.concatenate([k_prev, kc_ref[rows, :]], axis=0)
        vv = jnp.concatenate([v_prev, vc_ref[rows, :]], axis=0)
        q_blk = q_ref[rows, :]
        for hd in range(A_HEADS):
            g = hd // A_GROUP
            kcols = slice(g * A_HEAD_DIM, (g + 1) * A_HEAD_DIM)
            hcols = slice(hd * A_HEAD_DIM, (hd + 1) * A_HEAD_DIM)
            s = lax.dot_general(q_blk[:, hcols], kk[:, kcols], (((1,), (1,)), ((), ())),
                                preferred_element_type=F32)
            s = s + bias_ref[hd]
            if r == 0:
                s = jnp.where(pad_mask, NEG, s)
            sink = sinks_ref[hd]
            m = jnp.maximum(jnp.max(s, axis=-1, keepdims=True), sink)
            p = jnp.exp(s - m)
            denom = jnp.sum(p, axis=-1, keepdims=True) + jnp.exp(sink - m)
            pv = jnp.dot(p.astype(BF16), vv[:, kcols], preferred_element_type=F32)
            o_ref[rows, hcols] = pv / denom


def _swa_call(sinks, table, bucket, qa, ka, va):
    b, s, _ = qa.shape
    rows = SWA_ROWS
    per = rows // BLK
    cur = lambda bi, i: (bi, i, 0)
    prev = lambda bi, i: (bi, jnp.maximum(i * per - 1, 0), 0)
    smem = pl.BlockSpec(memory_space=pltpu.SMEM)
    return pl.pallas_call(
        _swa_kernel,
        out_shape=jax.ShapeDtypeStruct((b, s, A_WIDTH), F32),
        grid=(b, s // rows),
        in_specs=[smem, smem, pl.BlockSpec(bucket.shape, lambda bi, i: (0, 0)),
                  pl.BlockSpec((None, rows, A_WIDTH), cur),
                  pl.BlockSpec((None, BLK, A_KV_WIDTH), prev),
                  pl.BlockSpec((None, rows, A_KV_WIDTH), cur),
                  pl.BlockSpec((None, BLK, A_KV_WIDTH), prev),
                  pl.BlockSpec((None, rows, A_KV_WIDTH), cur)],
        out_specs=pl.BlockSpec((None, rows, A_WIDTH), cur),
        scratch_shapes=[pltpu.VMEM((A_HEADS, BLK, 2 * BLK), F32)],
        compiler_params=pltpu.CompilerParams(
            dimension_semantics=("arbitrary", "arbitrary"), vmem_limit_bytes=VMEM_LIMIT),
        name="swa",
    )(sinks, table, bucket, qa, ka, ka, va, va)


def _mla_kernel(q_ref, k_ref, v_ref, o_ref, s0_ref, s1_ref, m_ref, acc_ref):
    qi = pl.program_id(2)
    tq, tk, ts = MLA_TQ, MLA_TK, MLA_TS
    streams = tq // ts

    def scores(c, dst_ref):
        start = pl.multiple_of(c * tk, tk)
        k = k_ref[pl.ds(start, tk), :]
        for h in range(streams):
            dst_ref[h] = lax.dot_general(q_ref[h * ts:(h + 1) * ts, :], k,
                                         (((1,), (1,)), ((), ())), preferred_element_type=F32)

    def accumulate(c, src_ref, diagonal):
        start = pl.multiple_of(c * tk, tk)
        for h in range(streams):
            nk = (h + 1) * ts if diagonal else tk
            v = v_ref[pl.ds(start, nk), :]
            v1 = jnp.concatenate([v, jnp.ones_like(v)], axis=1)

            def tile():
                s = src_ref[h, :, 0:nk]
                if diagonal:
                    qrow = h * ts + lax.broadcasted_iota(jnp.int32, (ts, nk), 0)
                    kcol = lax.broadcasted_iota(jnp.int32, (ts, nk), 1)
                    s = jnp.where(qrow >= kcol, s, NEG)
                return s

            m = m_ref[h]
            m_new = jnp.maximum(m, jnp.max(tile(), axis=-1, keepdims=True))
            alpha = jnp.exp2(m - m_new)
            p = jnp.exp2(tile() - m_new).astype(BF16)
            acc_ref[h] = alpha * acc_ref[h] + jnp.dot(p, v1, preferred_element_type=F32)
            m_ref[h] = m_new

    def tick(c, src_ref, dst_ref):
        scores(c + 1, dst_ref)
        accumulate(c, src_ref, False)

    m_ref[...] = jnp.full(m_ref.shape, NEG, F32)
    acc_ref[...] = jnp.zeros(acc_ref.shape, F32)

    odd = (qi % 2) == 1

    @pl.when(odd)
    def _():
        scores(0, s1_ref)
        tick(0, s1_ref, s0_ref)

    @pl.when(jnp.logical_not(odd))
    def _():
        scores(0, s0_ref)

    first = qi % 2

    def pair(t, carry):
        c = first + 2 * t
        tick(c, s0_ref, s1_ref)
        tick(c + 1, s1_ref, s0_ref)
        return carry

    lax.fori_loop(0, qi // 2, pair, 0)

    accumulate(qi, s0_ref, True)
    for h in range(streams):
        acc = acc_ref[h]
        o_ref[h * ts:(h + 1) * ts, :] = acc[:, :V_DIM] / acc[:, V_DIM:]


def _mla_call(qm, km, vb):
    b, s, _ = qm.shape
    tq, tk, ts = MLA_TQ, MLA_TK, MLA_TS
    assert tq == tk and tq % ts == 0
    streams = tq // ts
    return pl.pallas_call(
        _mla_kernel,
        out_shape=jax.ShapeDtypeStruct((b, s, B_WIDTH), F32),
        grid=(b, B_HEADS, s // tq),
        in_specs=[pl.BlockSpec((None, tq, QK_PAD), lambda bi, h, i: (bi, i, h)),
                  pl.BlockSpec((None, s, QK_PAD), lambda bi, h, i: (bi, 0, h)),
                  pl.BlockSpec((None, s, V_DIM), lambda bi, h, i: (bi, 0, h))],
        out_specs=pl.BlockSpec((None, tq, V_DIM), lambda bi, h, i: (bi, i, h)),
        scratch_shapes=[pltpu.VMEM((streams, ts, tk), F32),
                        pltpu.VMEM((streams, ts, tk), F32),
                        pltpu.VMEM((streams, ts, 1), F32),
                        pltpu.VMEM((streams, ts, 2 * V_DIM), F32)],
        compiler_params=pltpu.CompilerParams(
            dimension_semantics=("arbitrary", "arbitrary", "arbitrary"),
            vmem_limit_bytes=VMEM_LIMIT),
        name="mla",
    )(qm, km, vb)


def _ffn_kernel(tiles_per_seq, x_ref, oa_ref, ob_ref, ag_ref, bg_ref, wout_ref, fg_ref,
                wup_ref, cw_ref, cb_ref, wdn_ref, ng_ref, o_ref,
                carry_ref, ug_ref, uv_ref, act_ref):
    tm, tf = FFN_TM, FFN_TF
    @pl.when((pl.program_id(0) % tiles_per_seq) == 0)
    def _():
        carry_ref[...] = jnp.zeros_like(carry_ref)

    mixed = jnp.concatenate([_rms(oa_ref[...]) * ag_ref[...],
                             _rms(ob_ref[...]) * bg_ref[...]], axis=-1)
    x1 = x_ref[...] + jnp.dot(mixed.astype(BF16), wout_ref[...], preferred_element_type=F32)
    h2 = (_rms(x1) * fg_ref[...]).astype(BF16)

    def conv(u_ref, u, cols):
        u_ref[0:SUBLANES, :] = carry_ref[:, cols]
        u_ref[SUBLANES:SUBLANES + tm, :] = u
        carry_ref[:, cols] = u_ref[tm:tm + SUBLANES, :]
        w = cw_ref[:, cols]
        y = (u_ref[SUBLANES - 2:SUBLANES - 2 + tm, :] * w[0:1]
             + u_ref[SUBLANES - 1:SUBLANES - 1 + tm, :] * w[1:2]
             + u * w[2:3])
        return y + cb_ref[:, cols]

    for j in range(D_FF // tf):
        gcols = slice(j * tf, (j + 1) * tf)
        vcols = slice(D_FF + j * tf, D_FF + (j + 1) * tf)
        ug = jnp.dot(h2, wup_ref[:, gcols], preferred_element_type=F32)
        uv = jnp.dot(h2, wup_ref[:, vcols], preferred_element_type=F32)
        gate = conv(ug_ref, ug, gcols)
        val = conv(uv_ref, uv, vcols)
        act_ref[:, gcols] = (gate * (1.0 / (1.0 + jnp.exp(-gate))) * val).astype(BF16)

    x2 = x1 + jnp.dot(act_ref[...], wdn_ref[...], preferred_element_type=F32)
    o_ref[...] = _rms(x2) * ng_ref[...]


def _ffn_call(x2, oa, ob, ag, bg, wout, fg, wup, cw, cb, wdn, ng, tiles_per_seq):
    t = x2.shape[0]
    tm = FFN_TM
    row = lambda i: (i, 0)
    fixed = lambda i: (0, 0)
    full = lambda a: pl.BlockSpec(a.shape, fixed, pipeline_mode=pl.Buffered(1))
    return pl.pallas_call(
        functools.partial(_ffn_kernel, tiles_per_seq),
        out_shape=jax.ShapeDtypeStruct((t, D_MODEL), F32),
        grid=(t // tm,),
        in_specs=[pl.BlockSpec((tm, D_MODEL), row), pl.BlockSpec((tm, A_WIDTH), row),
                  pl.BlockSpec((tm, B_WIDTH), row), full(ag), full(bg), full(wout), full(fg),
                  full(wup), full(cw), full(cb), full(wdn), full(ng)],
        out_specs=pl.BlockSpec((tm, D_MODEL), row),
        scratch_shapes=[pltpu.VMEM((SUBLANES, 2 * D_FF), F32),
                        pltpu.VMEM((tm + SUBLANES, FFN_TF), F32),
                        pltpu.VMEM((tm + SUBLANES, FFN_TF), F32),
                        pltpu.VMEM((tm, D_FF), BF16)],
        compiler_params=pltpu.CompilerParams(
            dimension_semantics=("arbitrary",), vmem_limit_bytes=VMEM_LIMIT),
        name="ffn",
    )(x2, oa, ob, ag, bg, wout, fg, wup, cw, cb, wdn, ng)


def _t5_bucket_matrix():
    q_idx = BLK + np.arange(BLK)
    k_idx = np.arange(2 * BLK)
    dist = q_idx[:, None] - k_idx[None, :]
    max_exact = NUM_BUCKETS // 2
    n = np.maximum(dist, 0)
    large = max_exact + (np.log(np.maximum(n, 1).astype(np.float32) / max_exact)
                         / math.log(T5_MAX_DIST / max_exact)
                         * (NUM_BUCKETS - max_exact)).astype(np.int32)
    large = np.minimum(large, NUM_BUCKETS - 1)
    bucket = np.where(n < max_exact, n, large)
    in_window = (dist >= 0) & (dist < WINDOW)
    return np.where(in_window, bucket, -1).astype(np.int32)


def _row(v):
    return v.reshape(1, -1).astype(F32)


def kernel(x, positions, rel_bias_table, attn_norm_g, w_in, sinks, q_norm_g, w_q_b, kv_norm_g,
           w_kv_b, a_out_norm_g, b_out_norm_g, w_out, ffn_norm_g, w_up, conv_w, conv_b, w_down,
           final_norm_g):
    b, s, d = x.shape
    t = b * s
    assert d == D_MODEL and s % max(PROJ_TM, SWA_ROWS, MLA_TQ, FFN_TM) == 0
    assert attn_norm_g.shape[0] == 1, "one trunk layer"
    l = 0

    zeros64 = jnp.zeros((D_MODEL, LANES - ROPE_DIM), F32)
    kp = w_in[l][:, _C_KPA:_C_KPA + ROPE_DIM]
    k1, k2 = kp[:, :ROPE_HALF], kp[:, ROPE_HALF:]
    w_in_x = jnp.concatenate([w_in[l][:, :_C_KPA], k1, k2, zeros64, k2, k1, zeros64],
                             axis=1).astype(BF16)
    wq = w_q_b[l].reshape(Q_LORA, B_HEADS, NOPE_DIM + ROPE_DIM)
    q1 = wq[:, :, NOPE_DIM:NOPE_DIM + ROPE_HALF]
    q2 = wq[:, :, NOPE_DIM + ROPE_HALF:]
    zq = jnp.zeros((Q_LORA, B_HEADS, LANES - ROPE_DIM), F32)
    wq_x = jnp.concatenate([wq[:, :, :NOPE_DIM], q1, q2, zq, q2, q1, zq], axis=2)
    wq_x = wq_x.reshape(Q_LORA, B_HEADS * _QB_HEAD).astype(BF16)
    wkv = w_kv_b[l].reshape(KV_LORA, B_HEADS, NOPE_DIM + V_DIM)
    wkv_x = jnp.concatenate([wkv[:, :, :NOPE_DIM].reshape(KV_LORA, -1),
                             wkv[:, :, NOPE_DIM:].reshape(KV_LORA, -1)], axis=1).astype(BF16)

    inv_freq = ROPE_THETA ** (-jnp.arange(0, ROPE_DIM, 2, dtype=F32) / ROPE_DIM)
    pad = jnp.zeros((LANES - ROPE_DIM,), F32)
    freq = jnp.concatenate([inv_freq, inv_freq, pad]).reshape(1, LANES)
    ones = jnp.ones((ROPE_HALF,), F32)
    sgn = jnp.concatenate([-ones, ones, pad]).reshape(1, LANES)

    x2 = x.reshape(t, d)
    pos2 = positions.reshape(t, 1).astype(jnp.int32)

    qa, ka, va, qm, km, vb = _proj_call(
        x2, pos2, _row(attn_norm_g[l]), w_in_x, _row(q_norm_g[l]), wq_x,
        _row(kv_norm_g[l]), wkv_x, freq, sgn)

    bucket = jnp.asarray(_t5_bucket_matrix())
    out_a = _swa_call(sinks[l].astype(F32), rel_bias_table.astype(F32), bucket,
                      qa.reshape(b, s, -1), ka.reshape(b, s, -1), va.reshape(b, s, -1))
    out_b = _mla_call(qm.reshape(b, s, -1), km.reshape(b, s, -1), vb.reshape(b, s, -1))

    cw = jnp.concatenate([conv_w[l], jnp.zeros((SUBLANES - CONV_W, 2 * D_FF), F32)], axis=0)
    out = _ffn_call(
        x2, out_a.reshape(t, -1), out_b.reshape(t, -1), _row(a_out_norm_g[l]),
        _row(b_out_norm_g[l]), w_out[l].astype(BF16), _row(ffn_norm_g[l]),
        w_up[l].astype(BF16), cw.astype(F32), _row(conv_b[l]), w_down[l].astype(BF16),
        _row(final_norm_g), s // FFN_TM)
    return out.reshape(b, s, d)
```

```python
import functools
import math

import numpy as np
import jax
import jax.numpy as jnp
from jax import lax
from jax.experimental import pallas as pl
from jax.experimental.pallas import tpu as pltpu

F32 = jnp.float32
BF16 = jnp.bfloat16

D_MODEL = 1024
A_HEADS = 8
A_KV_HEADS = 2
A_HEAD_DIM = 64
A_GROUP = A_HEADS // A_KV_HEADS
WINDOW = 128
BLK = WINDOW
A_WIDTH = A_HEADS * A_HEAD_DIM
A_KV_WIDTH = A_KV_HEADS * A_HEAD_DIM
NUM_BUCKETS = 32
T5_MAX_DIST = 128
B_HEADS = 4
Q_LORA = 256
KV_LORA = 128
NOPE_DIM = 128
ROPE_DIM = 64
ROPE_HALF = ROPE_DIM // 2
V_DIM = 128
ROPE_THETA = 10000.0
B_WIDTH = B_HEADS * V_DIM
D_FF = 2816
CONV_W = 3
EPS = 1e-6
NEG = -1e30
LOG2E = math.log2(math.e)

LANES = 128
SUBLANES = 8
QK_PAD = 256
VMEM_LIMIT = 56 * 1024 * 1024

PROJ_TM = 512
SWA_ROWS = 512
MLA_TQ = 1024
MLA_TK = 1024
MLA_TS = 512
FFN_TM = 512
FFN_TF = 256

_C_QA = 0
_C_KA = _C_QA + A_WIDTH
_C_VA = _C_KA + A_KV_WIDTH
_C_CQ = _C_VA + A_KV_WIDTH
_C_CKV = _C_CQ + Q_LORA
_C_KPA = _C_CKV + KV_LORA
_C_KPB = _C_KPA + LANES
_C_END = _C_KPB + LANES
_QB_HEAD = 3 * LANES


def _rms(x):
    return x * lax.rsqrt(jnp.mean(x * x, axis=-1, keepdims=True) + EPS)


def _proj_kernel(x_ref, pos_ref, g_ref, w_in_ref, qg_ref, wq_ref, kvg_ref, wkv_ref,
                 freq_ref, sgn_ref, qa_ref, ka_ref, va_ref, qm_ref, km_ref, vb_ref):
    h = _rms(x_ref[...]) * g_ref[...]
    proj = jnp.dot(h.astype(BF16), w_in_ref[...], preferred_element_type=F32)
    qa_ref[...] = (proj[:, _C_QA:_C_KA] * (A_HEAD_DIM ** -0.5 * LOG2E)).astype(BF16)
    half = LANES // 2
    for dst_ref, c0 in ((ka_ref, _C_KA), (va_ref, _C_VA)):
        nat = proj[:, c0:c0 + A_KV_WIDTH]
        dst_ref[:, :A_KV_WIDTH] = nat.astype(BF16)
        dst_ref[:, A_KV_WIDTH:] = pltpu.roll(nat, half, 1).astype(BF16)

    ang = pos_ref[...].astype(F32) * freq_ref[...]
    cc = jnp.cos(ang)
    ss = jnp.sin(ang) * sgn_ref[...]
    kpe = (proj[:, _C_KPA:_C_KPB] * cc + proj[:, _C_KPB:_C_END] * ss).astype(BF16)

    qn = _rms(proj[:, _C_CQ:_C_CKV]) * qg_ref[...]
    qb = jnp.dot(qn.astype(BF16), wq_ref[...], preferred_element_type=F32)
    kvn = _rms(proj[:, _C_CKV:_C_KPA]) * kvg_ref[...]
    kv = jnp.dot(kvn.astype(BF16), wkv_ref[...], preferred_element_type=F32)

    scale = (NOPE_DIM + ROPE_DIM) ** -0.5 * LOG2E
    for hd in range(B_HEADS):
        c0 = hd * _QB_HEAD
        nope = qb[:, c0:c0 + LANES]
        qa_part = qb[:, c0 + LANES:c0 + 2 * LANES]
        qb_part = qb[:, c0 + 2 * LANES:c0 + 3 * LANES]
        o0 = hd * QK_PAD
        qm_ref[:, o0:o0 + LANES] = (nope * scale).astype(BF16)
        qm_ref[:, o0 + LANES:o0 + 2 * LANES] = ((qa_part * cc + qb_part * ss) * scale).astype(BF16)
        km_ref[:, o0:o0 + LANES] = kv[:, hd * NOPE_DIM:(hd + 1) * NOPE_DIM].astype(BF16)
        km_ref[:, o0 + LANES:o0 + 2 * LANES] = kpe
    vb_ref[...] = kv[:, B_HEADS * NOPE_DIM:].astype(BF16)


def _proj_call(x2, pos2, g, w_in, qg, wq, kvg, wkv, freq, sgn):
    t = x2.shape[0]
    tm = PROJ_TM
    row = lambda i: (i, 0)
    fixed = lambda i: (0, 0)
    full = lambda a: pl.BlockSpec(a.shape, fixed)
    out_shape = (
        jax.ShapeDtypeStruct((t, A_WIDTH), BF16),
        jax.ShapeDtypeStruct((t, 2 * A_KV_WIDTH), BF16),
        jax.ShapeDtypeStruct((t, 2 * A_KV_WIDTH), BF16),
        jax.ShapeDtypeStruct((t, B_HEADS * QK_PAD), BF16),
        jax.ShapeDtypeStruct((t, B_HEADS * QK_PAD), BF16),
        jax.ShapeDtypeStruct((t, B_WIDTH), BF16),
    )
    return pl.pallas_call(
        _proj_kernel,
        out_shape=out_shape,
        grid=(t // tm,),
        in_specs=[pl.BlockSpec((tm, D_MODEL), row), pl.BlockSpec((tm, 1), row),
                  full(g), full(w_in), full(qg), full(wq), full(kvg), full(wkv),
                  full(freq), full(sgn)],
        out_specs=tuple(pl.BlockSpec((tm, s.shape[1]), row) for s in out_shape),
        compiler_params=pltpu.CompilerParams(
            dimension_semantics=("arbitrary",), vmem_limit_bytes=VMEM_LIMIT),
        name="proj",
    )(x2, pos2, g, w_in, qg, wq, kvg, wkv, freq, sgn)


def _swa_kernel(sinks_ref, tbl_ref, bucket_ref, q_ref, kp_ref, kc_ref, vp_ref, vc_ref,
                o_ref, bias_ref):
    step = pl.program_id(1)
    half = LANES // 2

    @pl.when((pl.program_id(0) == 0) & (step == 0))
    def _():
        bk = bucket_ref[...]
        for hd in range(A_HEADS):
            acc = jnp.full((BLK, 2 * BLK), NEG, F32)
            for b in range(NUM_BUCKETS):
                acc = jnp.where(bk == b, tbl_ref[b, hd] * LOG2E, acc)
            bias_ref[hd] = acc

    kidx = lax.broadcasted_iota(jnp.int32, (BLK, 2 * BLK), 1)
    pad_mask = kidx < jnp.where(step == 0, BLK, 0)

    def lane_mask(n_rows, low):
        lane = lax.broadcasted_iota(jnp.int32, (n_rows, LANES), 1)
        return jnp.where((lane < half) == low, 1.0, 0.0).astype(BF16)

    n_keys = BLK + SWA_ROWS
    lo, hi = lane_mask(n_keys, True), lane_mask(n_keys, False)
    ones_low, ones_high = lane_mask(2 * BLK, True), lane_mask(2 * BLK, False)
    lo_q = lax.broadcasted_iota(jnp.int32, (BLK, LANES), 1) < half

    k_all = jnp.concatenate([kp_ref[...], kc_ref[...]], axis=0)
    v_all = jnp.concatenate([vp_ref[...], vc_ref[...]], axis=0)
    k_nat, k_swp = k_all[:, :LANES], k_all[:, LANES:]
    v_nat, v_swp = v_all[:, :LANES], v_all[:, LANES:]
    k_low, k_high = (k_nat * lo, k_swp * lo), (k_swp * hi, k_nat * hi)
    v_low, v_high = (v_nat * lo, v_swp * lo), (v_swp * hi, v_nat * hi)

    dims = (((1,), (1,)), ((), ()))
    for r in range(SWA_ROWS // BLK):
        rows = slice(r * BLK, (r + 1) * BLK)
        keys = slice(r * BLK, (r + 2) * BLK)
        q_blk = q_ref[rows, :]
        for g in range(A_KV_HEADS):
            qg = jnp.concatenate([q_blk[:, (2 * g) * LANES:(2 * g + 1) * LANES],
                                  q_blk[:, (2 * g + 1) * LANES:(2 * g + 2) * LANES]], axis=0)
            s_pair = (lax.dot_general(qg, k_low[g][keys], dims, preferred_element_type=F32),
                      lax.dot_general(qg, k_high[g][keys], dims, preferred_element_type=F32))
            rhs = jnp.concatenate(
                [jnp.concatenate([v_low[g][keys], ones_low], axis=1),
                 jnp.concatenate([v_high[g][keys], ones_high], axis=1)], axis=0)
            for j in range(A_GROUP // 2):
                probs, sink_terms = [], []
                for e in range(2):
                    hd = A_GROUP * g + 2 * j + e
                    s = s_pair[e][j * BLK:(j + 1) * BLK] + bias_ref[hd]
                    if r == 0:
                        s = jnp.where(pad_mask, NEG, s)
                    sink = sinks_ref[hd] * LOG2E
                    m = jnp.maximum(jnp.max(s, axis=-1, keepdims=True), sink)
                    probs.append(jnp.exp2(s - m).astype(BF16))
                    sink_terms.append(jnp.exp2(sink - m))
                res = jnp.dot(jnp.concatenate(probs, axis=1), rhs, preferred_element_type=F32)
                denom = res[:, LANES:] + jnp.where(lo_q, sink_terms[0], sink_terms[1])
                pair = 2 * g + j
                o_ref[rows, pair * LANES:(pair + 1) * LANES] = res[:, :LANES] / denom


def _swa_call(sinks, table, bucket, qa, kx, vx):
    b, s, _ = qa.shape
    rows = SWA_ROWS
    per = rows // BLK
    cur = lambda bi, i: (bi, i, 0)
    prev = lambda bi, i: (bi, jnp.maximum(i * per - 1, 0), 0)
    smem = pl.BlockSpec(memory_space=pltpu.SMEM)
    return pl.pallas_call(
        _swa_kernel,
        out_shape=jax.ShapeDtypeStruct((b, s, A_WIDTH), F32),
        grid=(b, s // rows),
        in_specs=[smem, smem, pl.BlockSpec(bucket.shape, lambda bi, i: (0, 0)),
                  pl.BlockSpec((None, rows, A_WIDTH), cur),
                  pl.BlockSpec((None, BLK, 2 * A_KV_WIDTH), prev),
                  pl.BlockSpec((None, rows, 2 * A_KV_WIDTH), cur),
                  pl.BlockSpec((None, BLK, 2 * A_KV_WIDTH), prev),
                  pl.BlockSpec((None, rows, 2 * A_KV_WIDTH), cur)],
        out_specs=pl.BlockSpec((None, rows, A_WIDTH), cur),
        scratch_shapes=[pltpu.VMEM((A_HEADS, BLK, 2 * BLK), F32)],
        compiler_params=pltpu.CompilerParams(
            dimension_semantics=("arbitrary", "arbitrary"), vmem_limit_bytes=VMEM_LIMIT),
        name="swa",
    )(sinks, table, bucket, qa, kx, kx, vx, vx)


def _mla_kernel(q_ref, k_ref, v_ref, o_ref, s0_ref, s1_ref, x0_ref, x1_ref, m_ref, acc_ref):
    qi = pl.program_id(2)
    tq, tk, ts = MLA_TQ, MLA_TK, MLA_TS
    streams = tq // ts
    buf0 = (s0_ref, x0_ref)
    buf1 = (s1_ref, x1_ref)

    def scores(c, dst):
        s_ref, x_ref = dst
        start = pl.multiple_of(c * tk, tk)
        k = k_ref[pl.ds(start, tk), :]
        for h in range(streams):
            s = lax.dot_general(q_ref[h * ts:(h + 1) * ts, :], k,
                                (((1,), (1,)), ((), ())), preferred_element_type=F32)
            s_ref[h] = s
            blk = s[:, 0:LANES]
            for j in range(1, tk // LANES):
                blk = jnp.maximum(blk, s[:, j * LANES:(j + 1) * LANES])
            x_ref[h] = blk

    def accumulate(c, src, diagonal):
        s_ref, x_ref = src
        start = pl.multiple_of(c * tk, tk)
        for h in range(streams):
            nk = (h + 1) * ts if diagonal else tk
            v = v_ref[pl.ds(start, nk), :]
            v1 = jnp.concatenate([v, jnp.ones_like(v)], axis=1)
            s = s_ref[h, :, 0:nk]
            if diagonal:
                qrow = h * ts + lax.broadcasted_iota(jnp.int32, (ts, nk), 0)
                kcol = lax.broadcasted_iota(jnp.int32, (ts, nk), 1)
                s = jnp.where(qrow >= kcol, s, NEG)
                row_max = jnp.max(s, axis=-1, keepdims=True)
            else:
                row_max = jnp.max(x_ref[h], axis=-1, keepdims=True)
            m = m_ref[h]
            m_new = jnp.maximum(m, row_max)
            alpha = jnp.exp2(m - m_new)
            p = jnp.exp2(s - m_new).astype(BF16)
            acc_ref[h] = alpha * acc_ref[h] + jnp.dot(p, v1, preferred_element_type=F32)
            m_ref[h] = m_new

    def tick(c, src, dst):
        scores(c + 1, dst)
        accumulate(c, src, False)

    m_ref[...] = jnp.full(m_ref.shape, NEG, F32)
    acc_ref[...] = jnp.zeros(acc_ref.shape, F32)

    odd = (qi % 2) == 1

    @pl.when(odd)
    def _():
        scores(0, buf1)
        tick(0, buf1, buf0)

    @pl.when(jnp.logical_not(odd))
    def _():
        scores(0, buf0)

    first = qi % 2

    def pair(t, carry):
        c = first + 2 * t
        tick(c, buf0, buf1)
        tick(c + 1, buf1, buf0)
        return carry

    lax.fori_loop(0, qi // 2, pair, 0)

    accumulate(qi, buf0, True)
    for h in range(streams):
        acc = acc_ref[h]
        o_ref[h * ts:(h + 1) * ts, :] = acc[:, :V_DIM] / acc[:, V_DIM:]


def _mla_call(qm, km, vb):
    b, s, _ = qm.shape
    tq, tk, ts = MLA_TQ, MLA_TK, MLA_TS
    assert tq == tk and tq % ts == 0
    streams = tq // ts
    return pl.pallas_call(
        _mla_kernel,
        out_shape=jax.ShapeDtypeStruct((b, s, B_WIDTH), F32),
        grid=(b, B_HEADS, s // tq),
        in_specs=[pl.BlockSpec((None, tq, QK_PAD), lambda bi, h, i: (bi, i, h)),
                  pl.BlockSpec((None, s, QK_PAD), lambda bi, h, i: (bi, 0, h)),
                  pl.BlockSpec((None, s, V_DIM), lambda bi, h, i: (bi, 0, h))],
        out_specs=pl.BlockSpec((None, tq, V_DIM), lambda bi, h, i: (bi, i, h)),
        scratch_shapes=[pltpu.VMEM((streams, ts, tk), F32),
                        pltpu.VMEM((streams, ts, tk), F32),
                        pltpu.VMEM((streams, ts, LANES), F32),
                        pltpu.VMEM((streams, ts, LANES), F32),
                        pltpu.VMEM((streams, ts, 1), F32),
                        pltpu.VMEM((streams, ts, 2 * V_DIM), F32)],
        compiler_params=pltpu.CompilerParams(
            dimension_semantics=("arbitrary", "arbitrary", "arbitrary"),
            vmem_limit_bytes=VMEM_LIMIT),
        name="mla",
    )(qm, km, vb)


def _ffn_kernel(tiles_per_seq, x_ref, oa_ref, ob_ref, ag_ref, bg_ref, wout_ref, fg_ref,
                wup_ref, cw_ref, cb_ref, wdn_ref, ng_ref, o_ref,
                carry_ref, ug_ref, uv_ref, act_ref):
    tm, tf = FFN_TM, FFN_TF
    @pl.when((pl.program_id(0) % tiles_per_seq) == 0)
    def _():
        carry_ref[...] = jnp.zeros_like(carry_ref)

    mixed = jnp.concatenate([_rms(oa_ref[...]) * ag_ref[...],
                             _rms(ob_ref[...]) * bg_ref[...]], axis=-1)
    x1 = x_ref[...] + jnp.dot(mixed.astype(BF16), wout_ref[...], preferred_element_type=F32)
    h2 = (_rms(x1) * fg_ref[...]).astype(BF16)

    def conv(u_ref, u, cols):
        u_ref[0:SUBLANES, :] = carry_ref[:, cols]
        u_ref[SUBLANES:SUBLANES + tm, :] = u
        carry_ref[:, cols] = u_ref[tm:tm + SUBLANES, :]
        w = cw_ref[:, cols]
        y = (u_ref[SUBLANES - 2:SUBLANES - 2 + tm, :] * w[0:1]
             + u_ref[SUBLANES - 1:SUBLANES - 1 + tm, :] * w[1:2]
             + u * w[2:3])
        return y + cb_ref[:, cols]

    for j in range(D_FF // tf):
        gcols = slice(j * tf, (j + 1) * tf)
        vcols = slice(D_FF + j * tf, D_FF + (j + 1) * tf)
        ug = jnp.dot(h2, wup_ref[:, gcols], preferred_element_type=F32)
        uv = jnp.dot(h2, wup_ref[:, vcols], preferred_element_type=F32)
        gate = conv(ug_ref, ug, gcols)
        val = conv(uv_ref, uv, vcols)
        act_ref[:, gcols] = (gate * (1.0 / (1.0 + jnp.exp(-gate))) * val).astype(BF16)

    x2 = x1 + jnp.dot(act_ref[...], wdn_ref[...], preferred_element_type=F32)
    o_ref[...] = _rms(x2) * ng_ref[...]


def _ffn_call(x2, oa, ob, ag, bg, wout, fg, wup, cw, cb, wdn, ng, tiles_per_seq):
    t = x2.shape[0]
    tm = FFN_TM
    row = lambda i: (i, 0)
    fixed = lambda i: (0, 0)
    full = lambda a: pl.BlockSpec(a.shape, fixed, pipeline_mode=pl.Buffered(1))
    return pl.pallas_call(
        functools.partial(_ffn_kernel, tiles_per_seq),
        out_shape=jax.ShapeDtypeStruct((t, D_MODEL), F32),
        grid=(t // tm,),
        in_specs=[pl.BlockSpec((tm, D_MODEL), row), pl.BlockSpec((tm, A_WIDTH), row),
                  pl.BlockSpec((tm, B_WIDTH), row), full(ag), full(bg), full(wout), full(fg),
                  full(wup), full(cw), full(cb), full(wdn), full(ng)],
        out_specs=pl.BlockSpec((tm, D_MODEL), row),
        scratch_shapes=[pltpu.VMEM((SUBLANES, 2 * D_FF), F32),
                        pltpu.VMEM((tm + SUBLANES, FFN_TF), F32),
                        pltpu.VMEM((tm + SUBLANES, FFN_TF), F32),
                        pltpu.VMEM((tm, D_FF), BF16)],
        compiler_params=pltpu.CompilerParams(
            dimension_semantics=("arbitrary",), vmem_limit_bytes=VMEM_LIMIT),
        name="ffn",
    )(x2, oa, ob, ag, bg, wout, fg, wup, cw, cb, wdn, ng)


def _t5_bucket_matrix():
    q_idx = BLK + np.arange(BLK)
    k_idx = np.arange(2 * BLK)
    dist = q_idx[:, None] - k_idx[None, :]
    max_exact = NUM_BUCKETS // 2
    n = np.maximum(dist, 0)
    large = max_exact + (np.log(np.maximum(n, 1).astype(np.float32) / max_exact)
                         / math.log(T5_MAX_DIST / max_exact)
                         * (NUM_BUCKETS - max_exact)).astype(np.int32)
    large = np.minimum(large, NUM_BUCKETS - 1)
    bucket = np.where(n < max_exact, n, large)
    in_window = (dist >= 0) & (dist < WINDOW)
    return np.where(in_window, bucket, -1).astype(np.int32)


def _row(v):
    return v.reshape(1, -1).astype(F32)


def kernel(x, positions, rel_bias_table, attn_norm_g, w_in, sinks, q_norm_g, w_q_b, kv_norm_g,
           w_kv_b, a_out_norm_g, b_out_norm_g, w_out, ffn_norm_g, w_up, conv_w, conv_b, w_down,
           final_norm_g):
    b, s, d = x.shape
    t = b * s
    assert d == D_MODEL and s % max(PROJ_TM, SWA_ROWS, MLA_TQ, FFN_TM) == 0
    assert attn_norm_g.shape[0] == 1, "one trunk layer"
    l = 0

    zeros64 = jnp.zeros((D_MODEL, LANES - ROPE_DIM), F32)
    kp = w_in[l][:, _C_KPA:_C_KPA + ROPE_DIM]
    k1, k2 = kp[:, :ROPE_HALF], kp[:, ROPE_HALF:]
    w_in_x = jnp.concatenate([w_in[l][:, :_C_KPA], k1, k2, zeros64, k2, k1, zeros64],
                             axis=1).astype(BF16)
    wq = w_q_b[l].reshape(Q_LORA, B_HEADS, NOPE_DIM + ROPE_DIM)
    q1 = wq[:, :, NOPE_DIM:NOPE_DIM + ROPE_HALF]
    q2 = wq[:, :, NOPE_DIM + ROPE_HALF:]
    zq = jnp.zeros((Q_LORA, B_HEADS, LANES - ROPE_DIM), F32)
    wq_x = jnp.concatenate([wq[:, :, :NOPE_DIM], q1, q2, zq, q2, q1, zq], axis=2)
    wq_x = wq_x.reshape(Q_LORA, B_HEADS * _QB_HEAD).astype(BF16)
    wkv = w_kv_b[l].reshape(KV_LORA, B_HEADS, NOPE_DIM + V_DIM)
    wkv_x = jnp.concatenate([wkv[:, :, :NOPE_DIM].reshape(KV_LORA, -1),
                             wkv[:, :, NOPE_DIM:].reshape(KV_LORA, -1)], axis=1).astype(BF16)

    inv_freq = ROPE_THETA ** (-jnp.arange(0, ROPE_DIM, 2, dtype=F32) / ROPE_DIM)
    pad = jnp.zeros((LANES - ROPE_DIM,), F32)
    freq = jnp.concatenate([inv_freq, inv_freq, pad]).reshape(1, LANES)
    ones = jnp.ones((ROPE_HALF,), F32)
    sgn = jnp.concatenate([-ones, ones, pad]).reshape(1, LANES)

    x2 = x.reshape(t, d)
    pos2 = positions.reshape(t, 1).astype(jnp.int32)

    qa, ka, va, qm, km, vb = _proj_call(
        x2, pos2, _row(attn_norm_g[l]), w_in_x, _row(q_norm_g[l]), wq_x,
        _row(kv_norm_g[l]), wkv_x, freq, sgn)

    bucket = jnp.asarray(_t5_bucket_matrix())
    out_a = _swa_call(sinks[l].astype(F32), rel_bias_table.astype(F32), bucket,
                      qa.reshape(b, s, -1), ka.reshape(b, s, -1), va.reshape(b, s, -1))
    out_b = _mla_call(qm.reshape(b, s, -1), km.reshape(b, s, -1), vb.reshape(b, s, -1))

    cw = jnp.concatenate([conv_w[l], jnp.zeros((SUBLANES - CONV_W, 2 * D_FF), F32)], axis=0)
    out = _ffn_call(
        x2, out_a.reshape(t, -1), out_b.reshape(t, -1), _row(a_out_norm_g[l]),
        _row(b_out_norm_g[l]), w_out[l].astype(BF16), _row(ffn_norm_g[l]),
        w_up[l].astype(BF16), cw.astype(F32), _row(conv_b[l]), w_down[l].astype(BF16),
        _row(final_norm_g), s // FFN_TM)
    return out.reshape(b, s, d)
```

```python
import functools
import math

import numpy as np
import jax
import jax.numpy as jnp
from jax import lax
from jax.experimental import pallas as pl
from jax.experimental.pallas import tpu as pltpu

F32 = jnp.float32
BF16 = jnp.bfloat16

D_MODEL = 1024
A_HEADS = 8
A_KV_HEADS = 2
A_HEAD_DIM = 64
A_GROUP = A_HEADS // A_KV_HEADS
WINDOW = 128
BLK = WINDOW
A_WIDTH = A_HEADS * A_HEAD_DIM
A_KV_WIDTH = A_KV_HEADS * A_HEAD_DIM
NUM_BUCKETS = 32
T5_MAX_DIST = 128
B_HEADS = 4
Q_LORA = 256
KV_LORA = 128
NOPE_DIM = 128
ROPE_DIM = 64
ROPE_HALF = ROPE_DIM // 2
V_DIM = 128
ROPE_THETA = 10000.0
B_WIDTH = B_HEADS * V_DIM
D_FF = 2816
CONV_W = 3
EPS = 1e-6
NEG = -1e30
LOG2E = math.log2(math.e)

LANES = 128
SUBLANES = 8
QK_PAD = 256
VMEM_LIMIT = 56 * 1024 * 1024

PROJ_TM = 1024
PROJ_TS = 512
SWA_ROWS = 1024
MLA_TQ = 1024
MLA_TK = 1024
MLA_TS = 512
MLA_UNROLL = 8
FFN_TM = 512
FFN_TF = 256

_C_QA = 0
_C_KA = _C_QA + A_WIDTH
_C_VA = _C_KA + A_KV_WIDTH
_C_CQ = _C_VA + A_KV_WIDTH
_C_CKV = _C_CQ + Q_LORA
_C_KPA = _C_CKV + KV_LORA
_C_KPB = _C_KPA + LANES
_C_END = _C_KPB + LANES
_QB_HEAD = 3 * LANES


def _rms(x):
    return x * lax.rsqrt(jnp.mean(x * x, axis=-1, keepdims=True) + EPS)


def _proj_kernel(x_ref, pos_ref, g_ref, w_in_ref, qg_ref, wq_ref, kvg_ref, wkv_ref,
                 freq_ref, qa_ref, ka_ref, va_ref, qm_ref, km_ref, vb_ref,
                 cos_ref, sin_ref):
    quarter = PROJ_TM // 4
    lane = lax.broadcasted_iota(jnp.int32, (quarter, LANES), 1)
    pos_q = [pos_ref[j * quarter:(j + 1) * quarter, :].astype(F32) for j in range(4)]
    pos_d = jnp.where(lane < ROPE_HALF, pos_q[0],
                      jnp.where(lane < 2 * ROPE_HALF, pos_q[1],
                                jnp.where(lane < 3 * ROPE_HALF, pos_q[2], pos_q[3])))
    ang = pos_d * freq_ref[...]
    for table, dst_ref, sign in ((jnp.cos(ang), cos_ref, 1.0), (jnp.sin(ang), sin_ref, -1.0)):
        for j in range(4):
            at0 = pltpu.roll(table, (LANES - j * ROPE_HALF) % LANES, 1)
            at1 = pltpu.roll(table, (LANES + ROPE_HALF - j * ROPE_HALF) % LANES, 1)
            dst_ref[j * quarter:(j + 1) * quarter, :] = jnp.where(lane < ROPE_HALF,
                                                                  sign * at0, at1)

    for st in range(PROJ_TM // PROJ_TS):
        r = slice(st * PROJ_TS, (st + 1) * PROJ_TS)
        h = _rms(x_ref[r, :]) * g_ref[...]
        proj = jnp.dot(h.astype(BF16), w_in_ref[...], preferred_element_type=F32)
        qa_ref[r, :] = (proj[:, _C_QA:_C_KA] * (A_HEAD_DIM ** -0.5 * LOG2E)).astype(BF16)
        half = LANES // 2
        for dst_ref, c0 in ((ka_ref, _C_KA), (va_ref, _C_VA)):
            nat = proj[:, c0:c0 + A_KV_WIDTH]
            dst_ref[r, :A_KV_WIDTH] = nat.astype(BF16)
            dst_ref[r, A_KV_WIDTH:] = pltpu.roll(nat, half, 1).astype(BF16)

        cc = cos_ref[r, :]
        ss = sin_ref[r, :]
        kpe = (proj[:, _C_KPA:_C_KPB] * cc + proj[:, _C_KPB:_C_END] * ss).astype(BF16)

        qn = _rms(proj[:, _C_CQ:_C_CKV]) * qg_ref[...]
        qb = jnp.dot(qn.astype(BF16), wq_ref[...], preferred_element_type=F32)
        kvn = _rms(proj[:, _C_CKV:_C_KPA]) * kvg_ref[...]
        kv = jnp.dot(kvn.astype(BF16), wkv_ref[...], preferred_element_type=F32)

        scale = (NOPE_DIM + ROPE_DIM) ** -0.5 * LOG2E
        for hd in range(B_HEADS):
            c0 = hd * _QB_HEAD
            nope = qb[:, c0:c0 + LANES]
            qa_part = qb[:, c0 + LANES:c0 + 2 * LANES]
            qb_part = qb[:, c0 + 2 * LANES:c0 + 3 * LANES]
            o0 = hd * QK_PAD
            qm_ref[r, o0:o0 + LANES] = (nope * scale).astype(BF16)
            qm_ref[r, o0 + LANES:o0 + 2 * LANES] = (
                (qa_part * cc + qb_part * ss) * scale).astype(BF16)
            km_ref[r, o0:o0 + LANES] = kv[:, hd * NOPE_DIM:(hd + 1) * NOPE_DIM].astype(BF16)
            km_ref[r, o0 + LANES:o0 + 2 * LANES] = kpe
        vb_ref[r, :] = kv[:, B_HEADS * NOPE_DIM:].astype(BF16)


def _proj_call(x2, pos2, g, w_in, qg, wq, kvg, wkv, freq):
    t = x2.shape[0]
    tm = PROJ_TM
    row = lambda i: (i, 0)
    fixed = lambda i: (0, 0)
    full = lambda a: pl.BlockSpec(a.shape, fixed)
    out_shape = (
        jax.ShapeDtypeStruct((t, A_WIDTH), BF16),
        jax.ShapeDtypeStruct((t, 2 * A_KV_WIDTH), BF16),
        jax.ShapeDtypeStruct((t, 2 * A_KV_WIDTH), BF16),
        jax.ShapeDtypeStruct((t, B_HEADS * QK_PAD), BF16),
        jax.ShapeDtypeStruct((t, B_HEADS * QK_PAD), BF16),
        jax.ShapeDtypeStruct((t, B_WIDTH), BF16),
    )
    return pl.pallas_call(
        _proj_kernel,
        out_shape=out_shape,
        grid=(t // tm,),
        in_specs=[pl.BlockSpec((tm, D_MODEL), row), pl.BlockSpec((tm, 1), row),
                  full(g), full(w_in), full(qg), full(wq), full(kvg), full(wkv),
                  full(freq)],
        out_specs=tuple(pl.BlockSpec((tm, s.shape[1]), row) for s in out_shape),
        scratch_shapes=[pltpu.VMEM((tm, LANES), F32), pltpu.VMEM((tm, LANES), F32)],
        compiler_params=pltpu.CompilerParams(
            dimension_semantics=("arbitrary",), vmem_limit_bytes=VMEM_LIMIT),
        name="proj",
    )(x2, pos2, g, w_in, qg, wq, kvg, wkv, freq)


def _swa_kernel(sinks_ref, tbl_ref, bucket_ref, q_ref, kp_ref, kc_ref, vp_ref, vc_ref,
                o_ref, bias_ref, p_ref, t_ref):
    step = pl.program_id(1)
    half = LANES // 2

    @pl.when((pl.program_id(0) == 0) & (step == 0))
    def _():
        bk = bucket_ref[...]
        for hd in range(A_HEADS):
            acc = jnp.full((BLK, 2 * BLK), NEG, F32)
            for b in range(NUM_BUCKETS):
                acc = jnp.where(bk == b, tbl_ref[b, hd] * LOG2E, acc)
            bias_ref[hd] = acc

    kidx = lax.broadcasted_iota(jnp.int32, (BLK, 2 * BLK), 1)
    pad_mask = kidx < jnp.where(step == 0, BLK, 0)

    def lane_mask(n_rows, low):
        lane = lax.broadcasted_iota(jnp.int32, (n_rows, LANES), 1)
        return jnp.where((lane < half) == low, 1.0, 0.0).astype(BF16)

    n_keys = BLK + SWA_ROWS
    lo, hi = lane_mask(n_keys, True), lane_mask(n_keys, False)
    ones_low, ones_high = lane_mask(2 * BLK, True), lane_mask(2 * BLK, False)
    lo_q = lax.broadcasted_iota(jnp.int32, (BLK, LANES), 1) < half

    k_all = jnp.concatenate([kp_ref[...], kc_ref[...]], axis=0)
    v_all = jnp.concatenate([vp_ref[...], vc_ref[...]], axis=0)
    k_nat, k_swp = k_all[:, :LANES], k_all[:, LANES:]
    v_nat, v_swp = v_all[:, :LANES], v_all[:, LANES:]
    k_low, k_high = (k_nat * lo, k_swp * lo), (k_swp * hi, k_nat * hi)
    v_low, v_high = (v_nat * lo, v_swp * lo), (v_swp * hi, v_nat * hi)

    dims = (((1,), (1,)), ((), ()))
    n_blk = SWA_ROWS // BLK
    pairs = A_GROUP // 2
    for r in range(n_blk):
        keys = slice(r * BLK, (r + 2) * BLK)
        q_blk = q_ref[r * BLK:(r + 1) * BLK, :]
        for g in range(A_KV_HEADS):
            qg = jnp.concatenate([q_blk[:, (2 * g) * LANES:(2 * g + 1) * LANES],
                                  q_blk[:, (2 * g + 1) * LANES:(2 * g + 2) * LANES]], axis=0)
            s_pair = [lax.dot_general(qg, k_side[g][keys], dims, preferred_element_type=F32)
                      for k_side in (k_low, k_high)]
            for j in range(pairs):
                pair = (r * A_KV_HEADS + g) * pairs + j
                sink_terms = []
                for e in range(2):
                    hd = A_GROUP * g + 2 * j + e
                    s = s_pair[e][j * BLK:(j + 1) * BLK, :]
                    s = s + bias_ref[hd]
                    if r == 0:
                        s = jnp.where(pad_mask, NEG, s)
                    sink = sinks_ref[hd] * LOG2E
                    m = jnp.maximum(jnp.max(s, axis=-1, keepdims=True), sink)
                    p_ref[pair, :, e * 2 * BLK:(e + 1) * 2 * BLK] = jnp.exp2(s - m).astype(BF16)
                    sink_terms.append(jnp.exp2(sink - m))
                t_ref[pair] = jnp.where(lo_q, sink_terms[0], sink_terms[1])

    for r in range(n_blk):
        rows = slice(r * BLK, (r + 1) * BLK)
        keys = slice(r * BLK, (r + 2) * BLK)
        for g in range(A_KV_HEADS):
            rhs = jnp.concatenate(
                [jnp.concatenate([v_low[g][keys], ones_low], axis=1),
                 jnp.concatenate([v_high[g][keys], ones_high], axis=1)], axis=0)
            for j in range(pairs):
                pair = (r * A_KV_HEADS + g) * pairs + j
                res = jnp.dot(p_ref[pair], rhs, preferred_element_type=F32)
                col = (g * pairs + j) * LANES
                o_ref[rows, col:col + LANES] = res[:, :LANES] / (res[:, LANES:] + t_ref[pair])


def _swa_call(sinks, table, bucket, qa, kx, vx):
    b, s, _ = qa.shape
    rows = SWA_ROWS
    per = rows // BLK
    cur = lambda bi, i: (bi, i, 0)
    prev = lambda bi, i: (bi, jnp.maximum(i * per - 1, 0), 0)
    smem = pl.BlockSpec(memory_space=pltpu.SMEM)
    return pl.pallas_call(
        _swa_kernel,
        out_shape=jax.ShapeDtypeStruct((b, s, A_WIDTH), F32),
        grid=(b, s // rows),
        in_specs=[smem, smem, pl.BlockSpec(bucket.shape, lambda bi, i: (0, 0)),
                  pl.BlockSpec((None, rows, A_WIDTH), cur),
                  pl.BlockSpec((None, BLK, 2 * A_KV_WIDTH), prev),
                  pl.BlockSpec((None, rows, 2 * A_KV_WIDTH), cur),
                  pl.BlockSpec((None, BLK, 2 * A_KV_WIDTH), prev),
                  pl.BlockSpec((None, rows, 2 * A_KV_WIDTH), cur)],
        out_specs=pl.BlockSpec((None, rows, A_WIDTH), cur),
        scratch_shapes=[pltpu.VMEM((A_HEADS, BLK, 2 * BLK), F32),
                        pltpu.VMEM((per * A_HEADS // 2, BLK, 4 * BLK), BF16),
                        pltpu.VMEM((per * A_HEADS // 2, BLK, LANES), F32)],
        compiler_params=pltpu.CompilerParams(
            dimension_semantics=("arbitrary", "arbitrary"), vmem_limit_bytes=VMEM_LIMIT),
        name="swa",
    )(sinks, table, bucket, qa, kx, kx, vx, vx)


def _mla_kernel(q_ref, k_ref, v_ref, o_ref, s0_ref, s1_ref, x0_ref, x1_ref, m_ref, acc_ref):
    qi = pl.program_id(2)
    tq, tk, ts = MLA_TQ, MLA_TK, MLA_TS
    streams = tq // ts
    buf0 = (s0_ref, x0_ref)
    buf1 = (s1_ref, x1_ref)

    def scores(c, dst):
        s_ref, x_ref = dst
        start = pl.multiple_of(c * tk, tk)
        k = k_ref[pl.ds(start, tk), :]
        for h in range(streams):
            s = lax.dot_general(q_ref[h * ts:(h + 1) * ts, :], k,
                                (((1,), (1,)), ((), ())), preferred_element_type=F32)
            s_ref[h] = s
            blk = s[:, 0:LANES]
            for j in range(1, tk // LANES):
                blk = jnp.maximum(blk, s[:, j * LANES:(j + 1) * LANES])
            x_ref[h] = blk

    def accumulate(c, src, diagonal):
        s_ref, x_ref = src
        start = pl.multiple_of(c * tk, tk)
        for h in range(streams):
            nk = (h + 1) * ts if diagonal else tk
            v = v_ref[pl.ds(start, nk), :]
            v1 = jnp.concatenate([v, jnp.ones_like(v)], axis=1)
            s = s_ref[h, :, 0:nk]
            if diagonal:
                qrow = h * ts + lax.broadcasted_iota(jnp.int32, (ts, nk), 0)
                kcol = lax.broadcasted_iota(jnp.int32, (ts, nk), 1)
                s = jnp.where(qrow >= kcol, s, NEG)
                row_max = jnp.max(s, axis=-1, keepdims=True)
            else:
                row_max = jnp.max(x_ref[h], axis=-1, keepdims=True)
            m = m_ref[h]
            m_new = jnp.maximum(m, row_max)
            alpha = jnp.exp2(m - m_new)
            p = jnp.exp2(s - m_new).astype(BF16)
            acc_ref[h] = alpha * acc_ref[h] + jnp.dot(p, v1, preferred_element_type=F32)
            m_ref[h] = m_new

    def tick(c, src, dst):
        scores(c + 1, dst)
        accumulate(c, src, False)

    m_ref[...] = jnp.full(m_ref.shape, NEG, F32)
    acc_ref[...] = jnp.zeros(acc_ref.shape, F32)

    odd = (qi % 2) == 1

    @pl.when(odd)
    def _():
        scores(0, buf1)
        tick(0, buf1, buf0)

    @pl.when(jnp.logical_not(odd))
    def _():
        scores(0, buf0)

    def ticks(c, count):
        for i in range(0, count, 2):
            tick(c + i, buf0, buf1)
            tick(c + i + 1, buf1, buf0)

    done = qi % 2
    size = 2
    while size < MLA_UNROLL:
        pl.when((qi & size) != 0)(functools.partial(ticks, done, size))
        done = done + (qi & size)
        size *= 2

    def group(t, carry):
        ticks(done + MLA_UNROLL * t, MLA_UNROLL)
        return carry

    lax.fori_loop(0, qi // MLA_UNROLL, group, 0)

    accumulate(qi, buf0, True)
    for h in range(streams):
        acc = acc_ref[h]
        o_ref[h * ts:(h + 1) * ts, :] = acc[:, :V_DIM] / acc[:, V_DIM:]


def _mla_call(qm, km, vb):
    b, s, _ = qm.shape
    tq, tk, ts = MLA_TQ, MLA_TK, MLA_TS
    assert tq == tk and tq % ts == 0
    streams = tq // ts
    return pl.pallas_call(
        _mla_kernel,
        out_shape=jax.ShapeDtypeStruct((b, s, B_WIDTH), F32),
        grid=(b, B_HEADS, s // tq),
        in_specs=[pl.BlockSpec((None, tq, QK_PAD), lambda bi, h, i: (bi, i, h)),
                  pl.BlockSpec((None, s, QK_PAD), lambda bi, h, i: (bi, 0, h)),
                  pl.BlockSpec((None, s, V_DIM), lambda bi, h, i: (bi, 0, h))],
        out_specs=pl.BlockSpec((None, tq, V_DIM), lambda bi, h, i: (bi, i, h)),
        scratch_shapes=[pltpu.VMEM((streams, ts, tk), F32),
                        pltpu.VMEM((streams, ts, tk), F32),
                        pltpu.VMEM((streams, ts, LANES), F32),
                        pltpu.VMEM((streams, ts, LANES), F32),
                        pltpu.VMEM((streams, ts, 1), F32),
                        pltpu.VMEM((streams, ts, 2 * V_DIM), F32)],
        compiler_params=pltpu.CompilerParams(
            dimension_semantics=("arbitrary", "arbitrary", "arbitrary"),
            vmem_limit_bytes=VMEM_LIMIT),
        name="mla",
    )(qm, km, vb)


def _ffn_kernel(tiles_per_seq, x_ref, oa_ref, ob_ref, ag_ref, bg_ref, wout_ref, fg_ref,
                wup_ref, cw_ref, cb_ref, wdn_ref, ng_ref, o_ref,
                carry_ref, ug_ref, uv_ref, act_ref):
    tm, tf = FFN_TM, FFN_TF
    @pl.when((pl.program_id(0) % tiles_per_seq) == 0)
    def _():
        carry_ref[...] = jnp.zeros_like(carry_ref)

    mixed = jnp.concatenate([_rms(oa_ref[...]) * ag_ref[...],
                             _rms(ob_ref[...]) * bg_ref[...]], axis=-1)
    x1 = x_ref[...] + jnp.dot(mixed.astype(BF16), wout_ref[...], preferred_element_type=F32)
    h2 = (_rms(x1) * fg_ref[...]).astype(BF16)

    def conv(u_ref, u, cols):
        u_ref[0:SUBLANES, :] = carry_ref[:, cols]
        u_ref[SUBLANES:SUBLANES + tm, :] = u
        carry_ref[:, cols] = u_ref[tm:tm + SUBLANES, :]
        w = cw_ref[:, cols]
        y = (u_ref[SUBLANES - 2:SUBLANES - 2 + tm, :] * w[0:1]
             + u_ref[SUBLANES - 1:SUBLANES - 1 + tm, :] * w[1:2]
             + u * w[2:3])
        return y + cb_ref[:, cols]

    for j in range(D_FF // tf):
        gcols = slice(j * tf, (j + 1) * tf)
        vcols = slice(D_FF + j * tf, D_FF + (j + 1) * tf)
        ug = jnp.dot(h2, wup_ref[:, gcols], preferred_element_type=F32)
        uv = jnp.dot(h2, wup_ref[:, vcols], preferred_element_type=F32)
        gate = conv(ug_ref, ug, gcols)
        val = conv(uv_ref, uv, vcols)
        act_ref[:, gcols] = (gate * (1.0 / (1.0 + jnp.exp(-gate))) * val).astype(BF16)

    x2 = x1 + jnp.dot(act_ref[...], wdn_ref[...], preferred_element_type=F32)
    o_ref[...] = _rms(x2) * ng_ref[...]


def _ffn_call(x2, oa, ob, ag, bg, wout, fg, wup, cw, cb, wdn, ng, tiles_per_seq):
    t = x2.shape[0]
    tm = FFN_TM
    row = lambda i: (i, 0)
    fixed = lambda i: (0, 0)
    full = lambda a: pl.BlockSpec(a.shape, fixed, pipeline_mode=pl.Buffered(1))
    return pl.pallas_call(
        functools.partial(_ffn_kernel, tiles_per_seq),
        out_shape=jax.ShapeDtypeStruct((t, D_MODEL), F32),
        grid=(t // tm,),
        in_specs=[pl.BlockSpec((tm, D_MODEL), row), pl.BlockSpec((tm, A_WIDTH), row),
                  pl.BlockSpec((tm, B_WIDTH), row), full(ag), full(bg), full(wout), full(fg),
                  full(wup), full(cw), full(cb), full(wdn), full(ng)],
        out_specs=pl.BlockSpec((tm, D_MODEL), row),
        scratch_shapes=[pltpu.VMEM((SUBLANES, 2 * D_FF), F32),
                        pltpu.VMEM((tm + SUBLANES, FFN_TF), F32),
                        pltpu.VMEM((tm + SUBLANES, FFN_TF), F32),
                        pltpu.VMEM((tm, D_FF), BF16)],
        compiler_params=pltpu.CompilerParams(
            dimension_semantics=("arbitrary",), vmem_limit_bytes=VMEM_LIMIT),
        name="ffn",
    )(x2, oa, ob, ag, bg, wout, fg, wup, cw, cb, wdn, ng)


def _t5_bucket_matrix():
    q_idx = BLK + np.arange(BLK)
    k_idx = np.arange(2 * BLK)
    dist = q_idx[:, None] - k_idx[None, :]
    max_exact = NUM_BUCKETS // 2
    n = np.maximum(dist, 0)
    large = max_exact + (np.log(np.maximum(n, 1).astype(np.float32) / max_exact)
                         / math.log(T5_MAX_DIST / max_exact)
                         * (NUM_BUCKETS - max_exact)).astype(np.int32)
    large = np.minimum(large, NUM_BUCKETS - 1)
    bucket = np.where(n < max_exact, n, large)
    in_window = (dist >= 0) & (dist < WINDOW)
    return np.where(in_window, bucket, -1).astype(np.int32)


def _row(v):
    return v.reshape(1, -1).astype(F32)


def kernel(x, positions, rel_bias_table, attn_norm_g, w_in, sinks, q_norm_g, w_q_b, kv_norm_g,
           w_kv_b, a_out_norm_g, b_out_norm_g, w_out, ffn_norm_g, w_up, conv_w, conv_b, w_down,
           final_norm_g):
    b, s, d = x.shape
    t = b * s
    assert d == D_MODEL and s % max(PROJ_TM, SWA_ROWS, MLA_TQ, FFN_TM) == 0
    assert attn_norm_g.shape[0] == 1, "one trunk layer"
    l = 0

    zeros64 = jnp.zeros((D_MODEL, LANES - ROPE_DIM), F32)
    kp = w_in[l][:, _C_KPA:_C_KPA + ROPE_DIM]
    k1, k2 = kp[:, :ROPE_HALF], kp[:, ROPE_HALF:]
    w_in_x = jnp.concatenate([w_in[l][:, :_C_KPA], k1, k2, zeros64, k2, k1, zeros64],
                             axis=1).astype(BF16)
    wq = w_q_b[l].reshape(Q_LORA, B_HEADS, NOPE_DIM + ROPE_DIM)
    q1 = wq[:, :, NOPE_DIM:NOPE_DIM + ROPE_HALF]
    q2 = wq[:, :, NOPE_DIM + ROPE_HALF:]
    zq = jnp.zeros((Q_LORA, B_HEADS, LANES - ROPE_DIM), F32)
    wq_x = jnp.concatenate([wq[:, :, :NOPE_DIM], q1, q2, zq, q2, q1, zq], axis=2)
    wq_x = wq_x.reshape(Q_LORA, B_HEADS * _QB_HEAD).astype(BF16)
    wkv = w_kv_b[l].reshape(KV_LORA, B_HEADS, NOPE_DIM + V_DIM)
    wkv_x = jnp.concatenate([wkv[:, :, :NOPE_DIM].reshape(KV_LORA, -1),
                             wkv[:, :, NOPE_DIM:].reshape(KV_LORA, -1)], axis=1).astype(BF16)

    inv_freq = ROPE_THETA ** (-jnp.arange(0, ROPE_DIM, 2, dtype=F32) / ROPE_DIM)
    freq = jnp.tile(inv_freq, LANES // ROPE_HALF).reshape(1, LANES)

    x2 = x.reshape(t, d)
    pos2 = positions.reshape(t, 1).astype(jnp.int32)

    qa, ka, va, qm, km, vb = _proj_call(
        x2, pos2, _row(attn_norm_g[l]), w_in_x, _row(q_norm_g[l]), wq_x,
        _row(kv_norm_g[l]), wkv_x, freq)

    bucket = jnp.asarray(_t5_bucket_matrix())
    out_a = _swa_call(sinks[l].astype(F32), rel_bias_table.astype(F32), bucket,
                      qa.reshape(b, s, -1), ka.reshape(b, s, -1), va.reshape(b, s, -1))
    out_b = _mla_call(qm.reshape(b, s, -1), km.reshape(b, s, -1), vb.reshape(b, s, -1))

    cw = jnp.concatenate([conv_w[l], jnp.zeros((SUBLANES - CONV_W, 2 * D_FF), F32)], axis=0)
    out = _ffn_call(
        x2, out_a.reshape(t, -1), out_b.reshape(t, -1), _row(a_out_norm_g[l]),
        _row(b_out_norm_g[l]), w_out[l].astype(BF16), _row(ffn_norm_g[l]),
        w_up[l].astype(BF16), cw.astype(F32), _row(conv_b[l]), w_down[l].astype(BF16),
        _row(final_norm_g), s // FFN_TM)
    return out.reshape(b, s, d)
```

```python
import functools
import math

import numpy as np
import jax
import jax.numpy as jnp
from jax import lax
from jax.experimental import pallas as pl
from jax.experimental.pallas import tpu as pltpu

F32 = jnp.float32
BF16 = jnp.bfloat16

D_MODEL = 1024
A_HEADS = 8
A_KV_HEADS = 2
A_HEAD_DIM = 64
A_GROUP = A_HEADS // A_KV_HEADS
WINDOW = 128
BLK = WINDOW
A_WIDTH = A_HEADS * A_HEAD_DIM
A_KV_WIDTH = A_KV_HEADS * A_HEAD_DIM
NUM_BUCKETS = 32
T5_MAX_DIST = 128
B_HEADS = 4
Q_LORA = 256
KV_LORA = 128
NOPE_DIM = 128
ROPE_DIM = 64
ROPE_HALF = ROPE_DIM // 2
V_DIM = 128
ROPE_THETA = 10000.0
B_WIDTH = B_HEADS * V_DIM
D_FF = 2816
CONV_W = 3
EPS = 1e-6
NEG = -1e30
LOG2E = math.log2(math.e)

LANES = 128
SUBLANES = 8
QK_PAD = 256
VMEM_LIMIT = 56 * 1024 * 1024

PROJ_TM = 1024
PROJ_TS = 512
SWA_ROWS = 1024
MLA_TQ = 1024
MLA_TK = 1024
MLA_TS = 256
MLA_UNROLL = 8
FFN_TM = 512
FFN_TF = 256

_C_QA = 0
_C_KA = _C_QA + A_WIDTH
_C_VA = _C_KA + A_KV_WIDTH
_C_CQ = _C_VA + A_KV_WIDTH
_C_CKV = _C_CQ + Q_LORA
_C_KPA = _C_CKV + KV_LORA
_C_KPB = _C_KPA + LANES
_C_END = _C_KPB + LANES
_QB_HEAD = 3 * LANES


def _rms(x):
    return x * lax.rsqrt(jnp.mean(x * x, axis=-1, keepdims=True) + EPS)


def _proj_kernel(x_ref, pos_ref, g_ref, w_in_ref, qg_ref, wq_ref, kvg_ref, wkv_ref,
                 freq_ref, qa_ref, ka_ref, va_ref, qm_ref, km_ref, vb_ref,
                 cos_ref, sin_ref):
    quarter = PROJ_TM // 4
    lane = lax.broadcasted_iota(jnp.int32, (quarter, LANES), 1)
    pos_q = [pos_ref[j * quarter:(j + 1) * quarter, :].astype(F32) for j in range(4)]
    pos_d = jnp.where(lane < ROPE_HALF, pos_q[0],
                      jnp.where(lane < 2 * ROPE_HALF, pos_q[1],
                                jnp.where(lane < 3 * ROPE_HALF, pos_q[2], pos_q[3])))
    ang = pos_d * freq_ref[...]
    for table, dst_ref, sign in ((jnp.cos(ang), cos_ref, 1.0), (jnp.sin(ang), sin_ref, -1.0)):
        for j in range(4):
            at0 = pltpu.roll(table, (LANES - j * ROPE_HALF) % LANES, 1)
            at1 = pltpu.roll(table, (LANES + ROPE_HALF - j * ROPE_HALF) % LANES, 1)
            dst_ref[j * quarter:(j + 1) * quarter, :] = jnp.where(lane < ROPE_HALF,
                                                                  sign * at0, at1)

    for st in range(PROJ_TM // PROJ_TS):
        r = slice(st * PROJ_TS, (st + 1) * PROJ_TS)
        h = _rms(x_ref[r, :]) * g_ref[...]
        proj = jnp.dot(h.astype(BF16), w_in_ref[...], preferred_element_type=F32)
        qa_ref[r, :] = (proj[:, _C_QA:_C_KA] * (A_HEAD_DIM ** -0.5 * LOG2E)).astype(BF16)
        half = LANES // 2
        for dst_ref, c0 in ((ka_ref, _C_KA), (va_ref, _C_VA)):
            nat = proj[:, c0:c0 + A_KV_WIDTH]
            dst_ref[r, :A_KV_WIDTH] = nat.astype(BF16)
            dst_ref[r, A_KV_WIDTH:] = pltpu.roll(nat, half, 1).astype(BF16)

        cc = cos_ref[r, :]
        ss = sin_ref[r, :]
        kpe = (proj[:, _C_KPA:_C_KPB] * cc + proj[:, _C_KPB:_C_END] * ss).astype(BF16)

        qn = _rms(proj[:, _C_CQ:_C_CKV]) * qg_ref[...]
        qb = jnp.dot(qn.astype(BF16), wq_ref[...], preferred_element_type=F32)
        kvn = _rms(proj[:, _C_CKV:_C_KPA]) * kvg_ref[...]
        kv = jnp.dot(kvn.astype(BF16), wkv_ref[...], preferred_element_type=F32)

        scale = (NOPE_DIM + ROPE_DIM) ** -0.5 * LOG2E
        for hd in range(B_HEADS):
            c0 = hd * _QB_HEAD
            nope = qb[:, c0:c0 + LANES]
            qa_part = qb[:, c0 + LANES:c0 + 2 * LANES]
            qb_part = qb[:, c0 + 2 * LANES:c0 + 3 * LANES]
            o0 = hd * QK_PAD
            qm_ref[r, o0:o0 + LANES] = (nope * scale).astype(BF16)
            qm_ref[r, o0 + LANES:o0 + 2 * LANES] = (
                (qa_part * cc + qb_part * ss) * scale).astype(BF16)
            km_ref[r, o0:o0 + LANES] = kv[:, hd * NOPE_DIM:(hd + 1) * NOPE_DIM].astype(BF16)
            km_ref[r, o0 + LANES:o0 + 2 * LANES] = kpe
        vb_ref[r, :] = kv[:, B_HEADS * NOPE_DIM:].astype(BF16)


def _proj_call(x2, pos2, g, w_in, qg, wq, kvg, wkv, freq):
    t = x2.shape[0]
    tm = PROJ_TM
    row = lambda i: (i, 0)
    fixed = lambda i: (0, 0)
    full = lambda a: pl.BlockSpec(a.shape, fixed)
    out_shape = (
        jax.ShapeDtypeStruct((t, A_WIDTH), BF16),
        jax.ShapeDtypeStruct((t, 2 * A_KV_WIDTH), BF16),
        jax.ShapeDtypeStruct((t, 2 * A_KV_WIDTH), BF16),
        jax.ShapeDtypeStruct((t, B_HEADS * QK_PAD), BF16),
        jax.ShapeDtypeStruct((t, B_HEADS * QK_PAD), BF16),
        jax.ShapeDtypeStruct((t, B_WIDTH), BF16),
    )
    return pl.pallas_call(
        _proj_kernel,
        out_shape=out_shape,
        grid=(t // tm,),
        in_specs=[pl.BlockSpec((tm, D_MODEL), row), pl.BlockSpec((tm, 1), row),
                  full(g), full(w_in), full(qg), full(wq), full(kvg), full(wkv),
                  full(freq)],
        out_specs=tuple(pl.BlockSpec((tm, s.shape[1]), row) for s in out_shape),
        scratch_shapes=[pltpu.VMEM((tm, LANES), F32), pltpu.VMEM((tm, LANES), F32)],
        compiler_params=pltpu.CompilerParams(
            dimension_semantics=("arbitrary",), vmem_limit_bytes=VMEM_LIMIT),
        name="proj",
    )(x2, pos2, g, w_in, qg, wq, kvg, wkv, freq)


def _swa_kernel(sinks_ref, tbl_ref, bucket_ref, q_ref, kp_ref, kc_ref, vp_ref, vc_ref,
                o_ref, bias_ref, p_ref, t_ref):
    step = pl.program_id(1)
    half = LANES // 2

    @pl.when((pl.program_id(0) == 0) & (step == 0))
    def _():
        bk = bucket_ref[...]
        for hd in range(A_HEADS):
            acc = jnp.full((BLK, 2 * BLK), NEG, F32)
            for b in range(NUM_BUCKETS):
                acc = jnp.where(bk == b, tbl_ref[b, hd] * LOG2E, acc)
            bias_ref[hd] = acc

    kidx = lax.broadcasted_iota(jnp.int32, (BLK, 2 * BLK), 1)
    pad_mask = kidx < jnp.where(step == 0, BLK, 0)

    def lane_mask(n_rows, low):
        lane = lax.broadcasted_iota(jnp.int32, (n_rows, LANES), 1)
        return jnp.where((lane < half) == low, 1.0, 0.0).astype(BF16)

    n_keys = BLK + SWA_ROWS
    lo, hi = lane_mask(n_keys, True), lane_mask(n_keys, False)
    ones_low, ones_high = lane_mask(2 * BLK, True), lane_mask(2 * BLK, False)
    lo_q = lax.broadcasted_iota(jnp.int32, (BLK, LANES), 1) < half

    k_all = jnp.concatenate([kp_ref[...], kc_ref[...]], axis=0)
    v_all = jnp.concatenate([vp_ref[...], vc_ref[...]], axis=0)
    k_nat, k_swp = k_all[:, :LANES], k_all[:, LANES:]
    v_nat, v_swp = v_all[:, :LANES], v_all[:, LANES:]
    k_low, k_high = (k_nat * lo, k_swp * lo), (k_swp * hi, k_nat * hi)
    v_low, v_high = (v_nat * lo, v_swp * lo), (v_swp * hi, v_nat * hi)

    dims = (((1,), (1,)), ((), ()))
    n_blk = SWA_ROWS // BLK
    pairs = A_GROUP // 2
    for r in range(n_blk):
        keys = slice(r * BLK, (r + 2) * BLK)
        q_blk = q_ref[r * BLK:(r + 1) * BLK, :]
        for g in range(A_KV_HEADS):
            qg = jnp.concatenate([q_blk[:, (2 * g) * LANES:(2 * g + 1) * LANES],
                                  q_blk[:, (2 * g + 1) * LANES:(2 * g + 2) * LANES]], axis=0)
            s_pair = [lax.dot_general(qg, k_side[g][keys], dims, preferred_element_type=F32)
                      for k_side in (k_low, k_high)]
            for j in range(pairs):
                pair = (r * A_KV_HEADS + g) * pairs + j
                sink_terms = []
                for e in range(2):
                    hd = A_GROUP * g + 2 * j + e
                    s = s_pair[e][j * BLK:(j + 1) * BLK, :]
                    s = s + bias_ref[hd]
                    if r == 0:
                        s = jnp.where(pad_mask, NEG, s)
                    sink = sinks_ref[hd] * LOG2E
                    m = jnp.maximum(jnp.max(s, axis=-1, keepdims=True), sink)
                    p_ref[pair, :, e * 2 * BLK:(e + 1) * 2 * BLK] = jnp.exp2(s - m).astype(BF16)
                    sink_terms.append(jnp.exp2(sink - m))
                t_ref[pair] = jnp.where(lo_q, sink_terms[0], sink_terms[1])

    for r in range(n_blk):
        rows = slice(r * BLK, (r + 1) * BLK)
        keys = slice(r * BLK, (r + 2) * BLK)
        for g in range(A_KV_HEADS):
            rhs = jnp.concatenate(
                [jnp.concatenate([v_low[g][keys], ones_low], axis=1),
                 jnp.concatenate([v_high[g][keys], ones_high], axis=1)], axis=0)
            for j in range(pairs):
                pair = (r * A_KV_HEADS + g) * pairs + j
                res = jnp.dot(p_ref[pair], rhs, preferred_element_type=F32)
                col = (g * pairs + j) * LANES
                o_ref[rows, col:col + LANES] = res[:, :LANES] / (res[:, LANES:] + t_ref[pair])


def _swa_call(sinks, table, bucket, qa, kx, vx):
    b, s, _ = qa.shape
    rows = SWA_ROWS
    per = rows // BLK
    cur = lambda bi, i: (bi, i, 0)
    prev = lambda bi, i: (bi, jnp.maximum(i * per - 1, 0), 0)
    smem = pl.BlockSpec(memory_space=pltpu.SMEM)
    return pl.pallas_call(
        _swa_kernel,
        out_shape=jax.ShapeDtypeStruct((b, s, A_WIDTH), F32),
        grid=(b, s // rows),
        in_specs=[smem, smem, pl.BlockSpec(bucket.shape, lambda bi, i: (0, 0)),
                  pl.BlockSpec((None, rows, A_WIDTH), cur),
                  pl.BlockSpec((None, BLK, 2 * A_KV_WIDTH), prev),
                  pl.BlockSpec((None, rows, 2 * A_KV_WIDTH), cur),
                  pl.BlockSpec((None, BLK, 2 * A_KV_WIDTH), prev),
                  pl.BlockSpec((None, rows, 2 * A_KV_WIDTH), cur)],
        out_specs=pl.BlockSpec((None, rows, A_WIDTH), cur),
        scratch_shapes=[pltpu.VMEM((A_HEADS, BLK, 2 * BLK), F32),
                        pltpu.VMEM((per * A_HEADS // 2, BLK, 4 * BLK), BF16),
                        pltpu.VMEM((per * A_HEADS // 2, BLK, LANES), F32)],
        compiler_params=pltpu.CompilerParams(
            dimension_semantics=("arbitrary", "arbitrary"), vmem_limit_bytes=VMEM_LIMIT),
        name="swa",
    )(sinks, table, bucket, qa, kx, kx, vx, vx)


def _mla_kernel(q_ref, k_ref, v_ref, o_ref, s0_ref, s1_ref, x0_ref, x1_ref, m_ref, acc_ref):
    qi = pl.program_id(2)
    tq, tk, ts = MLA_TQ, MLA_TK, MLA_TS
    streams = tq // ts
    buf0 = (s0_ref, x0_ref)
    buf1 = (s1_ref, x1_ref)

    def scores(c, dst):
        s_ref, x_ref = dst
        start = pl.multiple_of(c * tk, tk)
        k = k_ref[pl.ds(start, tk), :]
        for h in range(streams):
            s = lax.dot_general(q_ref[h * ts:(h + 1) * ts, :], k,
                                (((1,), (1,)), ((), ())), preferred_element_type=F32)
            s_ref[h] = s
            blk = s[:, 0:LANES]
            for j in range(1, tk // LANES):
                blk = jnp.maximum(blk, s[:, j * LANES:(j + 1) * LANES])
            x_ref[h] = blk

    def accumulate(c, src, diagonal):
        s_ref, x_ref = src
        start = pl.multiple_of(c * tk, tk)
        for h in range(streams):
            nk = (h + 1) * ts if diagonal else tk
            v = v_ref[pl.ds(start, nk), :]
            v1 = jnp.concatenate([v, jnp.ones_like(v)], axis=1)
            s = s_ref[h, :, 0:nk]
            if diagonal:
                qrow = h * ts + lax.broadcasted_iota(jnp.int32, (ts, nk), 0)
                kcol = lax.broadcasted_iota(jnp.int32, (ts, nk), 1)
                s = jnp.where(qrow >= kcol, s, NEG)
                row_max = jnp.max(s, axis=-1, keepdims=True)
            else:
                row_max = jnp.max(x_ref[h], axis=-1, keepdims=True)
            m = m_ref[h]
            m_new = jnp.maximum(m, row_max)
            alpha = jnp.exp2(m - m_new)
            p = jnp.exp2(s - m_new).astype(BF16)
            acc_ref[h] = alpha * acc_ref[h] + jnp.dot(p, v1, preferred_element_type=F32)
            m_ref[h] = m_new

    def tick(c, src, dst):
        scores(c + 1, dst)
        accumulate(c, src, False)

    m_ref[...] = jnp.full(m_ref.shape, NEG, F32)
    acc_ref[...] = jnp.zeros(acc_ref.shape, F32)

    odd = (qi % 2) == 1

    @pl.when(odd)
    def _():
        scores(0, buf1)
        tick(0, buf1, buf0)

    @pl.when(jnp.logical_not(odd))
    def _():
        scores(0, buf0)

    def ticks(c, count):
        for i in range(0, count, 2):
            tick(c + i, buf0, buf1)
            tick(c + i + 1, buf1, buf0)

    done = qi % 2
    size = 2
    while size < MLA_UNROLL:
        pl.when((qi & size) != 0)(functools.partial(ticks, done, size))
        done = done + (qi & size)
        size *= 2

    def group(t, carry):
        ticks(done + MLA_UNROLL * t, MLA_UNROLL)
        return carry

    lax.fori_loop(0, qi // MLA_UNROLL, group, 0)

    accumulate(qi, buf0, True)
    for h in range(streams):
        acc = acc_ref[h]
        o_ref[h * ts:(h + 1) * ts, :] = acc[:, :V_DIM] / acc[:, V_DIM:]


def _mla_call(qm, km, vb):
    b, s, _ = qm.shape
    tq, tk, ts = MLA_TQ, MLA_TK, MLA_TS
    assert tq == tk and tq % ts == 0
    streams = tq // ts
    return pl.pallas_call(
        _mla_kernel,
        out_shape=jax.ShapeDtypeStruct((b, s, B_WIDTH), F32),
        grid=(b, B_HEADS, s // tq),
        in_specs=[pl.BlockSpec((None, tq, QK_PAD), lambda bi, h, i: (bi, i, h)),
                  pl.BlockSpec((None, s, QK_PAD), lambda bi, h, i: (bi, 0, h)),
                  pl.BlockSpec((None, s, V_DIM), lambda bi, h, i: (bi, 0, h))],
        out_specs=pl.BlockSpec((None, tq, V_DIM), lambda bi, h, i: (bi, i, h)),
        scratch_shapes=[pltpu.VMEM((streams, ts, tk), F32),
                        pltpu.VMEM((streams, ts, tk), F32),
                        pltpu.VMEM((streams, ts, LANES), F32),
                        pltpu.VMEM((streams, ts, LANES), F32),
                        pltpu.VMEM((streams, ts, 1), F32),
                        pltpu.VMEM((streams, ts, 2 * V_DIM), F32)],
        compiler_params=pltpu.CompilerParams(
            dimension_semantics=("arbitrary", "arbitrary", "arbitrary"),
            vmem_limit_bytes=VMEM_LIMIT),
        name="mla",
    )(qm, km, vb)


def _ffn_kernel(tiles_per_seq, x_ref, oa_ref, ob_ref, ag_ref, bg_ref, wout_ref, fg_ref,
                wup_ref, cw_ref, cb_ref, wdn_ref, ng_ref, o_ref,
                carry_ref, ug_ref, uv_ref, act_ref):
    tm, tf = FFN_TM, FFN_TF
    @pl.when((pl.program_id(0) % tiles_per_seq) == 0)
    def _():
        carry_ref[...] = jnp.zeros_like(carry_ref)

    mixed = jnp.concatenate([_rms(oa_ref[...]) * ag_ref[...],
                             _rms(ob_ref[...]) * bg_ref[...]], axis=-1)
    x1 = x_ref[...] + jnp.dot(mixed.astype(BF16), wout_ref[...], preferred_element_type=F32)
    h2 = (_rms(x1) * fg_ref[...]).astype(BF16)

    def conv(u_ref, u, cols):
        u_ref[0:SUBLANES, :] = carry_ref[:, cols]
        u_ref[SUBLANES:SUBLANES + tm, :] = u
        carry_ref[:, cols] = u_ref[tm:tm + SUBLANES, :]
        w = cw_ref[:, cols]
        y = (u_ref[SUBLANES - 2:SUBLANES - 2 + tm, :] * w[0:1]
             + u_ref[SUBLANES - 1:SUBLANES - 1 + tm, :] * w[1:2]
             + u * w[2:3])
        return y + cb_ref[:, cols]

    for j in range(D_FF // tf):
        gcols = slice(j * tf, (j + 1) * tf)
        vcols = slice(D_FF + j * tf, D_FF + (j + 1) * tf)
        ug = jnp.dot(h2, wup_ref[:, gcols], preferred_element_type=F32)
        uv = jnp.dot(h2, wup_ref[:, vcols], preferred_element_type=F32)
        gate = conv(ug_ref, ug, gcols)
        val = conv(uv_ref, uv, vcols)
        act_ref[:, gcols] = (gate * (1.0 / (1.0 + jnp.exp(-gate))) * val).astype(BF16)

    x2 = x1 + jnp.dot(act_ref[...], wdn_ref[...], preferred_element_type=F32)
    o_ref[...] = _rms(x2) * ng_ref[...]


def _ffn_call(x2, oa, ob, ag, bg, wout, fg, wup, cw, cb, wdn, ng, tiles_per_seq):
    t = x2.shape[0]
    tm = FFN_TM
    row = lambda i: (i, 0)
    fixed = lambda i: (0, 0)
    full = lambda a: pl.BlockSpec(a.shape, fixed, pipeline_mode=pl.Buffered(1))
    return pl.pallas_call(
        functools.partial(_ffn_kernel, tiles_per_seq),
        out_shape=jax.ShapeDtypeStruct((t, D_MODEL), F32),
        grid=(t // tm,),
        in_specs=[pl.BlockSpec((tm, D_MODEL), row), pl.BlockSpec((tm, A_WIDTH), row),
                  pl.BlockSpec((tm, B_WIDTH), row), full(ag), full(bg), full(wout), full(fg),
                  full(wup), full(cw), full(cb), full(wdn), full(ng)],
        out_specs=pl.BlockSpec((tm, D_MODEL), row),
        scratch_shapes=[pltpu.VMEM((SUBLANES, 2 * D_FF), F32),
                        pltpu.VMEM((tm + SUBLANES, FFN_TF), F32),
                        pltpu.VMEM((tm + SUBLANES, FFN_TF), F32),
                        pltpu.VMEM((tm, D_FF), BF16)],
        compiler_params=pltpu.CompilerParams(
            dimension_semantics=("arbitrary",), vmem_limit_bytes=VMEM_LIMIT),
        name="ffn",
    )(x2, oa, ob, ag, bg, wout, fg, wup, cw, cb, wdn, ng)


def _t5_bucket_matrix():
    q_idx = BLK + np.arange(BLK)
    k_idx = np.arange(2 * BLK)
    dist = q_idx[:, None] - k_idx[None, :]
    max_exact = NUM_BUCKETS // 2
    n = np.maximum(dist, 0)
    large = max_exact + (np.log(np.maximum(n, 1).astype(np.float32) / max_exact)
                         / math.log(T5_MAX_DIST / max_exact)
                         * (NUM_BUCKETS - max_exact)).astype(np.int32)
    large = np.minimum(large, NUM_BUCKETS - 1)
    bucket = np.where(n < max_exact, n, large)
    in_window = (dist >= 0) & (dist < WINDOW)
    return np.where(in_window, bucket, -1).astype(np.int32)


def _row(v):
    return v.reshape(1, -1).astype(F32)


def kernel(x, positions, rel_bias_table, attn_norm_g, w_in, sinks, q_norm_g, w_q_b, kv_norm_g,
           w_kv_b, a_out_norm_g, b_out_norm_g, w_out, ffn_norm_g, w_up, conv_w, conv_b, w_down,
           final_norm_g):
    b, s, d = x.shape
    t = b * s
    assert d == D_MODEL and s % max(PROJ_TM, SWA_ROWS, MLA_TQ, FFN_TM) == 0
    assert attn_norm_g.shape[0] == 1, "one trunk layer"
    l = 0

    zeros64 = jnp.zeros((D_MODEL, LANES - ROPE_DIM), F32)
    kp = w_in[l][:, _C_KPA:_C_KPA + ROPE_DIM]
    k1, k2 = kp[:, :ROPE_HALF], kp[:, ROPE_HALF:]
    w_in_x = jnp.concatenate([w_in[l][:, :_C_KPA], k1, k2, zeros64, k2, k1, zeros64],
                             axis=1).astype(BF16)
    wq = w_q_b[l].reshape(Q_LORA, B_HEADS, NOPE_DIM + ROPE_DIM)
    q1 = wq[:, :, NOPE_DIM:NOPE_DIM + ROPE_HALF]
    q2 = wq[:, :, NOPE_DIM + ROPE_HALF:]
    zq = jnp.zeros((Q_LORA, B_HEADS, LANES - ROPE_DIM), F32)
    wq_x = jnp.concatenate([wq[:, :, :NOPE_DIM], q1, q2, zq, q2, q1, zq], axis=2)
    wq_x = wq_x.reshape(Q_LORA, B_HEADS * _QB_HEAD).astype(BF16)
    wkv = w_kv_b[l].reshape(KV_LORA, B_HEADS, NOPE_DIM + V_DIM)
    wkv_x = jnp.concatenate([wkv[:, :, :NOPE_DIM].reshape(KV_LORA, -1),
                             wkv[:, :, NOPE_DIM:].reshape(KV_LORA, -1)], axis=1).astype(BF16)

    inv_freq = ROPE_THETA ** (-jnp.arange(0, ROPE_DIM, 2, dtype=F32) / ROPE_DIM)
    freq = jnp.tile(inv_freq, LANES // ROPE_HALF).reshape(1, LANES)

    x2 = x.reshape(t, d)
    pos2 = positions.reshape(t, 1).astype(jnp.int32)

    qa, ka, va, qm, km, vb = _proj_call(
        x2, pos2, _row(attn_norm_g[l]), w_in_x, _row(q_norm_g[l]), wq_x,
        _row(kv_norm_g[l]), wkv_x, freq)

    bucket = jnp.asarray(_t5_bucket_matrix())
    out_a = _swa_call(sinks[l].astype(F32), rel_bias_table.astype(F32), bucket,
                      qa.reshape(b, s, -1), ka.reshape(b, s, -1), va.reshape(b, s, -1))
    out_b = _mla_call(qm.reshape(b, s, -1), km.reshape(b, s, -1), vb.reshape(b, s, -1))

    cw = jnp.concatenate([conv_w[l], jnp.zeros((SUBLANES - CONV_W, 2 * D_FF), F32)], axis=0)
    out = _ffn_call(
        x2, out_a.reshape(t, -1), out_b.reshape(t, -1), _row(a_out_norm_g[l]),
        _row(b_out_norm_g[l]), w_out[l].astype(BF16), _row(ffn_norm_g[l]),
        w_up[l].astype(BF16), cw.astype(F32), _row(conv_b[l]), w_down[l].astype(BF16),
        _row(final_norm_g), s // FFN_TM)
    return out.reshape(b, s, d)
```

```python
import functools
import math

import numpy as np
import jax
import jax.numpy as jnp
from jax import lax
from jax.experimental import pallas as pl
from jax.experimental.pallas import tpu as pltpu

F32 = jnp.float32
BF16 = jnp.bfloat16

D_MODEL = 1024
A_HEADS = 8
A_KV_HEADS = 2
A_HEAD_DIM = 64
A_GROUP = A_HEADS // A_KV_HEADS
WINDOW = 128
BLK = WINDOW
A_WIDTH = A_HEADS * A_HEAD_DIM
A_KV_WIDTH = A_KV_HEADS * A_HEAD_DIM
NUM_BUCKETS = 32
T5_MAX_DIST = 128
B_HEADS = 4
Q_LORA = 256
KV_LORA = 128
NOPE_DIM = 128
ROPE_DIM = 64
ROPE_HALF = ROPE_DIM // 2
V_DIM = 128
ROPE_THETA = 10000.0
B_WIDTH = B_HEADS * V_DIM
D_FF = 2816
CONV_W = 3
EPS = 1e-6
NEG = -1e30
LOG2E = math.log2(math.e)

LANES = 128
SUBLANES = 8
QK_PAD = 256
VMEM_LIMIT = 56 * 1024 * 1024

PROJ_TM = 1024
PROJ_TS = 512
SWA_ROWS = 1024
MLA_TQ = 1024
MLA_TK = 1024
MLA_TS = 256
MLA_UNROLL = 8
FFN_TM = 512
FFN_TF = 256
FFN_SLAB = 256
FFN_U_BUFS = 4

_C_QA = 0
_C_KA = _C_QA + A_WIDTH
_C_VA = _C_KA + A_KV_WIDTH
_C_CQ = _C_VA + A_KV_WIDTH
_C_CKV = _C_CQ + Q_LORA
_C_KPA = _C_CKV + KV_LORA
_C_KPB = _C_KPA + LANES
_C_END = _C_KPB + LANES
_QB_HEAD = 3 * LANES


def _rms(x):
    return x * lax.rsqrt(jnp.mean(x * x, axis=-1, keepdims=True) + EPS)


def _proj_kernel(x_ref, pos_ref, g_ref, w_in_ref, qg_ref, wq_ref, kvg_ref, wkv_ref,
                 freq_ref, qa_ref, ka_ref, va_ref, qm_ref, km_ref, vb_ref,
                 cos_ref, sin_ref):
    quarter = PROJ_TM // 4
    lane = lax.broadcasted_iota(jnp.int32, (quarter, LANES), 1)
    pos_q = [pos_ref[j * quarter:(j + 1) * quarter, :].astype(F32) for j in range(4)]
    pos_d = jnp.where(lane < ROPE_HALF, pos_q[0],
                      jnp.where(lane < 2 * ROPE_HALF, pos_q[1],
                                jnp.where(lane < 3 * ROPE_HALF, pos_q[2], pos_q[3])))
    ang = pos_d * freq_ref[...]
    for table, dst_ref, sign in ((jnp.cos(ang), cos_ref, 1.0), (jnp.sin(ang), sin_ref, -1.0)):
        for j in range(4):
            at0 = pltpu.roll(table, (LANES - j * ROPE_HALF) % LANES, 1)
            at1 = pltpu.roll(table, (LANES + ROPE_HALF - j * ROPE_HALF) % LANES, 1)
            dst_ref[j * quarter:(j + 1) * quarter, :] = jnp.where(lane < ROPE_HALF,
                                                                  sign * at0, at1)

    for st in range(PROJ_TM // PROJ_TS):
        r = slice(st * PROJ_TS, (st + 1) * PROJ_TS)
        h = _rms(x_ref[r, :]) * g_ref[...]
        proj = jnp.dot(h.astype(BF16), w_in_ref[...], preferred_element_type=F32)
        qa_ref[r, :] = (proj[:, _C_QA:_C_KA] * (A_HEAD_DIM ** -0.5 * LOG2E)).astype(BF16)
        half = LANES // 2
        for dst_ref, c0 in ((ka_ref, _C_KA), (va_ref, _C_VA)):
            nat = proj[:, c0:c0 + A_KV_WIDTH]
            dst_ref[r, :A_KV_WIDTH] = nat.astype(BF16)
            dst_ref[r, A_KV_WIDTH:] = pltpu.roll(nat, half, 1).astype(BF16)

        cc = cos_ref[r, :]
        ss = sin_ref[r, :]
        kpe = (proj[:, _C_KPA:_C_KPB] * cc + proj[:, _C_KPB:_C_END] * ss).astype(BF16)

        qn = _rms(proj[:, _C_CQ:_C_CKV]) * qg_ref[...]
        qb = jnp.dot(qn.astype(BF16), wq_ref[...], preferred_element_type=F32)
        kvn = _rms(proj[:, _C_CKV:_C_KPA]) * kvg_ref[...]
        kv = jnp.dot(kvn.astype(BF16), wkv_ref[...], preferred_element_type=F32)

        scale = (NOPE_DIM + ROPE_DIM) ** -0.5 * LOG2E
        for hd in range(B_HEADS):
            c0 = hd * _QB_HEAD
            nope = qb[:, c0:c0 + LANES]
            qa_part = qb[:, c0 + LANES:c0 + 2 * LANES]
            qb_part = qb[:, c0 + 2 * LANES:c0 + 3 * LANES]
            o0 = hd * QK_PAD
            qm_ref[r, o0:o0 + LANES] = (nope * scale).astype(BF16)
            qm_ref[r, o0 + LANES:o0 + 2 * LANES] = (
                (qa_part * cc + qb_part * ss) * scale).astype(BF16)
            km_ref[r, o0:o0 + LANES] = kv[:, hd * NOPE_DIM:(hd + 1) * NOPE_DIM].astype(BF16)
            km_ref[r, o0 + LANES:o0 + 2 * LANES] = kpe
        vb_ref[r, :] = kv[:, B_HEADS * NOPE_DIM:].astype(BF16)


def _proj_call(x2, pos2, g, w_in, qg, wq, kvg, wkv, freq):
    t = x2.shape[0]
    tm = PROJ_TM
    row = lambda i: (i, 0)
    fixed = lambda i: (0, 0)
    full = lambda a: pl.BlockSpec(a.shape, fixed)
    out_shape = (
        jax.ShapeDtypeStruct((t, A_WIDTH), BF16),
        jax.ShapeDtypeStruct((t, 2 * A_KV_WIDTH), BF16),
        jax.ShapeDtypeStruct((t, 2 * A_KV_WIDTH), BF16),
        jax.ShapeDtypeStruct((t, B_HEADS * QK_PAD), BF16),
        jax.ShapeDtypeStruct((t, B_HEADS * QK_PAD), BF16),
        jax.ShapeDtypeStruct((t, B_WIDTH), BF16),
    )
    return pl.pallas_call(
        _proj_kernel,
        out_shape=out_shape,
        grid=(t // tm,),
        in_specs=[pl.BlockSpec((tm, D_MODEL), row), pl.BlockSpec((tm, 1), row),
                  full(g), full(w_in), full(qg), full(wq), full(kvg), full(wkv),
                  full(freq)],
        out_specs=tuple(pl.BlockSpec((tm, s.shape[1]), row) for s in out_shape),
        scratch_shapes=[pltpu.VMEM((tm, LANES), F32), pltpu.VMEM((tm, LANES), F32)],
        compiler_params=pltpu.CompilerParams(
            dimension_semantics=("arbitrary",), vmem_limit_bytes=VMEM_LIMIT),
        name="proj",
    )(x2, pos2, g, w_in, qg, wq, kvg, wkv, freq)


def _swa_kernel(sinks_ref, tbl_ref, bucket_ref, q_ref, kp_ref, kc_ref, vp_ref, vc_ref,
                o_ref, bias_ref, p_ref, t_ref):
    step = pl.program_id(1)
    half = LANES // 2

    @pl.when((pl.program_id(0) == 0) & (step == 0))
    def _():
        bk = bucket_ref[...]
        for hd in range(A_HEADS):
            acc = jnp.full((BLK, 2 * BLK), NEG, F32)
            for b in range(NUM_BUCKETS):
                acc = jnp.where(bk == b, tbl_ref[b, hd] * LOG2E, acc)
            bias_ref[hd] = acc

    kidx = lax.broadcasted_iota(jnp.int32, (BLK, 2 * BLK), 1)
    pad_mask = kidx < jnp.where(step == 0, BLK, 0)

    def lane_mask(n_rows, low):
        lane = lax.broadcasted_iota(jnp.int32, (n_rows, LANES), 1)
        return jnp.where((lane < half) == low, 1.0, 0.0).astype(BF16)

    n_keys = BLK + SWA_ROWS
    lo, hi = lane_mask(n_keys, True), lane_mask(n_keys, False)
    ones_low, ones_high = lane_mask(2 * BLK, True), lane_mask(2 * BLK, False)
    lo_q = lax.broadcasted_iota(jnp.int32, (BLK, LANES), 1) < half

    k_all = jnp.concatenate([kp_ref[...], kc_ref[...]], axis=0)
    v_all = jnp.concatenate([vp_ref[...], vc_ref[...]], axis=0)
    k_nat, k_swp = k_all[:, :LANES], k_all[:, LANES:]
    v_nat, v_swp = v_all[:, :LANES], v_all[:, LANES:]
    k_low, k_high = (k_nat * lo, k_swp * lo), (k_swp * hi, k_nat * hi)
    v_low, v_high = (v_nat * lo, v_swp * lo), (v_swp * hi, v_nat * hi)

    dims = (((1,), (1,)), ((), ()))
    n_blk = SWA_ROWS // BLK
    pairs = A_GROUP // 2
    for r in range(n_blk):
        keys = slice(r * BLK, (r + 2) * BLK)
        q_blk = q_ref[r * BLK:(r + 1) * BLK, :]
        for g in range(A_KV_HEADS):
            qg = jnp.concatenate([q_blk[:, (2 * g) * LANES:(2 * g + 1) * LANES],
                                  q_blk[:, (2 * g + 1) * LANES:(2 * g + 2) * LANES]], axis=0)
            s_pair = [lax.dot_general(qg, k_side[g][keys], dims, preferred_element_type=F32)
                      for k_side in (k_low, k_high)]
            for j in range(pairs):
                pair = (r * A_KV_HEADS + g) * pairs + j
                sink_terms = []
                for e in range(2):
                    hd = A_GROUP * g + 2 * j + e
                    s = s_pair[e][j * BLK:(j + 1) * BLK, :]
                    s = s + bias_ref[hd]
                    if r == 0:
                        s = jnp.where(pad_mask, NEG, s)
                    sink = sinks_ref[hd] * LOG2E
                    m = jnp.maximum(jnp.max(s, axis=-1, keepdims=True), sink)
                    p_ref[pair, :, e * 2 * BLK:(e + 1) * 2 * BLK] = jnp.exp2(s - m).astype(BF16)
                    sink_terms.append(jnp.exp2(sink - m))
                t_ref[pair] = jnp.where(lo_q, sink_terms[0], sink_terms[1])

    for r in range(n_blk):
        rows = slice(r * BLK, (r + 1) * BLK)
        keys = slice(r * BLK, (r + 2) * BLK)
        for g in range(A_KV_HEADS):
            rhs = jnp.concatenate(
                [jnp.concatenate([v_low[g][keys], ones_low], axis=1),
                 jnp.concatenate([v_high[g][keys], ones_high], axis=1)], axis=0)
            for j in range(pairs):
                pair = (r * A_KV_HEADS + g) * pairs + j
                res = jnp.dot(p_ref[pair], rhs, preferred_element_type=F32)
                col = (g * pairs + j) * LANES
                o_ref[rows, col:col + LANES] = res[:, :LANES] / (res[:, LANES:] + t_ref[pair])


def _swa_call(sinks, table, bucket, qa, kx, vx):
    b, s, _ = qa.shape
    rows = SWA_ROWS
    per = rows // BLK
    cur = lambda bi, i: (bi, i, 0)
    prev = lambda bi, i: (bi, jnp.maximum(i * per - 1, 0), 0)
    smem = pl.BlockSpec(memory_space=pltpu.SMEM)
    return pl.pallas_call(
        _swa_kernel,
        out_shape=jax.ShapeDtypeStruct((b, s, A_WIDTH), F32),
        grid=(b, s // rows),
        in_specs=[smem, smem, pl.BlockSpec(bucket.shape, lambda bi, i: (0, 0)),
                  pl.BlockSpec((None, rows, A_WIDTH), cur),
                  pl.BlockSpec((None, BLK, 2 * A_KV_WIDTH), prev),
                  pl.BlockSpec((None, rows, 2 * A_KV_WIDTH), cur),
                  pl.BlockSpec((None, BLK, 2 * A_KV_WIDTH), prev),
                  pl.BlockSpec((None, rows, 2 * A_KV_WIDTH), cur)],
        out_specs=pl.BlockSpec((None, rows, A_WIDTH), cur),
        scratch_shapes=[pltpu.VMEM((A_HEADS, BLK, 2 * BLK), F32),
                        pltpu.VMEM((per * A_HEADS // 2, BLK, 4 * BLK), BF16),
                        pltpu.VMEM((per * A_HEADS // 2, BLK, LANES), F32)],
        compiler_params=pltpu.CompilerParams(
            dimension_semantics=("arbitrary", "arbitrary"), vmem_limit_bytes=VMEM_LIMIT),
        name="swa",
    )(sinks, table, bucket, qa, kx, kx, vx, vx)


def _mla_kernel(q_ref, k_ref, v_ref, o_ref, s0_ref, s1_ref, x0_ref, x1_ref, m_ref, acc_ref):
    qi = pl.program_id(2)
    tq, tk, ts = MLA_TQ, MLA_TK, MLA_TS
    streams = tq // ts
    buf0 = (s0_ref, x0_ref)
    buf1 = (s1_ref, x1_ref)

    def scores(c, dst):
        s_ref, x_ref = dst
        start = pl.multiple_of(c * tk, tk)
        k = k_ref[pl.ds(start, tk), :]
        for h in range(streams):
            s = lax.dot_general(q_ref[h * ts:(h + 1) * ts, :], k,
                                (((1,), (1,)), ((), ())), preferred_element_type=F32)
            s_ref[h] = s
            blk = s[:, 0:LANES]
            for j in range(1, tk // LANES):
                blk = jnp.maximum(blk, s[:, j * LANES:(j + 1) * LANES])
            x_ref[h] = blk

    def accumulate(c, src, diagonal):
        s_ref, x_ref = src
        start = pl.multiple_of(c * tk, tk)
        for h in range(streams):
            nk = (h + 1) * ts if diagonal else tk
            v = v_ref[pl.ds(start, nk), :]
            v1 = jnp.concatenate([v, jnp.ones_like(v)], axis=1)
            s = s_ref[h, :, 0:nk]
            if diagonal:
                qrow = h * ts + lax.broadcasted_iota(jnp.int32, (ts, nk), 0)
                kcol = lax.broadcasted_iota(jnp.int32, (ts, nk), 1)
                s = jnp.where(qrow >= kcol, s, NEG)
                row_max = jnp.max(s, axis=-1, keepdims=True)
            else:
                row_max = jnp.max(x_ref[h], axis=-1, keepdims=True)
            m = m_ref[h]
            m_new = jnp.maximum(m, row_max)
            alpha = jnp.exp2(m - m_new)
            p = jnp.exp2(s - m_new).astype(BF16)
            acc_ref[h] = alpha * acc_ref[h] + jnp.dot(p, v1, preferred_element_type=F32)
            m_ref[h] = m_new

    def tick(c, src, dst):
        scores(c + 1, dst)
        accumulate(c, src, False)

    m_ref[...] = jnp.full(m_ref.shape, NEG, F32)
    acc_ref[...] = jnp.zeros(acc_ref.shape, F32)

    odd = (qi % 2) == 1

    @pl.when(odd)
    def _():
        scores(0, buf1)
        tick(0, buf1, buf0)

    @pl.when(jnp.logical_not(odd))
    def _():
        scores(0, buf0)

    def ticks(c, count):
        for i in range(0, count, 2):
            tick(c + i, buf0, buf1)
            tick(c + i + 1, buf1, buf0)

    done = qi % 2
    size = 2
    while size < MLA_UNROLL:
        pl.when((qi & size) != 0)(functools.partial(ticks, done, size))
        done = done + (qi & size)
        size *= 2

    def group(t, carry):
        ticks(done + MLA_UNROLL * t, MLA_UNROLL)
        return carry

    lax.fori_loop(0, qi // MLA_UNROLL, group, 0)

    accumulate(qi, buf0, True)
    for h in range(streams):
        acc = acc_ref[h]
        o_ref[h * ts:(h + 1) * ts, :] = acc[:, :V_DIM] / acc[:, V_DIM:]


def _mla_call(qm, km, vb):
    b, s, _ = qm.shape
    tq, tk, ts = MLA_TQ, MLA_TK, MLA_TS
    assert tq == tk and tq % ts == 0
    streams = tq // ts
    return pl.pallas_call(
        _mla_kernel,
        out_shape=jax.ShapeDtypeStruct((b, s, B_WIDTH), F32),
        grid=(b, B_HEADS, s // tq),
        in_specs=[pl.BlockSpec((None, tq, QK_PAD), lambda bi, h, i: (bi, i, h)),
                  pl.BlockSpec((None, s, QK_PAD), lambda bi, h, i: (bi, 0, h)),
                  pl.BlockSpec((None, s, V_DIM), lambda bi, h, i: (bi, 0, h))],
        out_specs=pl.BlockSpec((None, tq, V_DIM), lambda bi, h, i: (bi, i, h)),
        scratch_shapes=[pltpu.VMEM((streams, ts, tk), F32),
                        pltpu.VMEM((streams, ts, tk), F32),
                        pltpu.VMEM((streams, ts, LANES), F32),
                        pltpu.VMEM((streams, ts, LANES), F32),
                        pltpu.VMEM((streams, ts, 1), F32),
                        pltpu.VMEM((streams, ts, 2 * V_DIM), F32)],
        compiler_params=pltpu.CompilerParams(
            dimension_semantics=("arbitrary", "arbitrary", "arbitrary"),
            vmem_limit_bytes=VMEM_LIMIT),
        name="mla",
    )(qm, km, vb)


def _ffn_kernel(tiles_per_seq, x_ref, oa_ref, ob_ref, ag_ref, bg_ref, wout_ref, fg_ref,
                wup_ref, cw_ref, cb_ref, wdn_ref, ng_ref, o_ref,
                carry_ref, u_ref, act0_ref, act1_ref, xa_ref, xb_ref):
    tm, tf = FFN_TM, FFN_TF
    step = pl.program_id(0)
    n_chunks = D_FF // tf
    n_slabs = D_MODEL // FFN_SLAB
    slab_after = {int((q + 0.5) * n_chunks / n_slabs): (q,) for q in range(n_slabs)}
    assert len(slab_after) == n_slabs

    @pl.when(step == 0)
    def _():
        act1_ref[...] = jnp.zeros_like(act1_ref)
        xb_ref[...] = jnp.zeros_like(xb_ref)

    @pl.when((step % tiles_per_seq) == 0)
    def _():
        carry_ref[...] = jnp.zeros_like(carry_ref)

    def conv(u_ref, u, cols):
        u_ref[0:SUBLANES, :] = carry_ref[:, cols]
        u_ref[SUBLANES:SUBLANES + tm, :] = u
        carry_ref[:, cols] = u_ref[tm:tm + SUBLANES, :]
        w = cw_ref[:, cols]
        y = (u_ref[SUBLANES - 2:SUBLANES - 2 + tm, :] * w[0:1]
             + u_ref[SUBLANES - 1:SUBLANES - 1 + tm, :] * w[1:2]
             + u * w[2:3])
        return y + cb_ref[:, cols]

    def body(act_w, x_w, act_r, x_r):
        mixed = jnp.concatenate([_rms(oa_ref[...]) * ag_ref[...],
                                 _rms(ob_ref[...]) * bg_ref[...]], axis=-1)
        x1 = x_ref[...] + jnp.dot(mixed.astype(BF16), wout_ref[...],
                                  preferred_element_type=F32)
        x_w[...] = x1
        h2 = (_rms(x1) * fg_ref[...]).astype(BF16)

        for j in range(n_chunks):
            gcols = slice(j * tf, (j + 1) * tf)
            vcols = slice(D_FF + j * tf, D_FF + (j + 1) * tf)
            ug = jnp.dot(h2, wup_ref[:, gcols], preferred_element_type=F32)
            uv = jnp.dot(h2, wup_ref[:, vcols], preferred_element_type=F32)
            gate = conv(u_ref.at[(2 * j) % FFN_U_BUFS], ug, gcols)
            val = conv(u_ref.at[(2 * j + 1) % FFN_U_BUFS], uv, vcols)
            act_w[:, gcols] = (gate * (1.0 / (1.0 + jnp.exp(-gate))) * val).astype(BF16)
            for q in slab_after.get(j, ()):
                cols = slice(q * FFN_SLAB, (q + 1) * FFN_SLAB)
                o_ref[:, cols] = x_r[:, cols] + jnp.dot(act_r[...], wdn_ref[:, cols],
                                                        preferred_element_type=F32)
        o_ref[...] = _rms(o_ref[...]) * ng_ref[...]

    pl.when(step % 2 == 0)(functools.partial(body, act0_ref, xa_ref, act1_ref, xb_ref))
    pl.when(step % 2 == 1)(functools.partial(body, act1_ref, xb_ref, act0_ref, xa_ref))


def _ffn_call(x2, oa, ob, ag, bg, wout, fg, wup, cw, cb, wdn, ng, tiles_per_seq):
    t = x2.shape[0]
    tm = FFN_TM
    n_tiles = t // tm
    row = lambda i: (jnp.minimum(i, n_tiles - 1), 0)
    out_row = lambda i: (jnp.maximum(i - 1, 0), 0)
    fixed = lambda i: (0, 0)
    full = lambda a: pl.BlockSpec(a.shape, fixed, pipeline_mode=pl.Buffered(1))
    return pl.pallas_call(
        functools.partial(_ffn_kernel, tiles_per_seq),
        out_shape=jax.ShapeDtypeStruct((t, D_MODEL), F32),
        grid=(n_tiles + 1,),
        in_specs=[pl.BlockSpec((tm, D_MODEL), row), pl.BlockSpec((tm, A_WIDTH), row),
                  pl.BlockSpec((tm, B_WIDTH), row), full(ag), full(bg), full(wout), full(fg),
                  full(wup), full(cw), full(cb), full(wdn), full(ng)],
        out_specs=pl.BlockSpec((tm, D_MODEL), out_row),
        scratch_shapes=[pltpu.VMEM((SUBLANES, 2 * D_FF), F32),
                        pltpu.VMEM((FFN_U_BUFS, tm + SUBLANES, FFN_TF), F32),
                        pltpu.VMEM((tm, D_FF), BF16),
                        pltpu.VMEM((tm, D_FF), BF16),
                        pltpu.VMEM((tm, D_MODEL), F32),
                        pltpu.VMEM((tm, D_MODEL), F32)],
        compiler_params=pltpu.CompilerParams(
            dimension_semantics=("arbitrary",), vmem_limit_bytes=VMEM_LIMIT),
        name="ffn",
    )(x2, oa, ob, ag, bg, wout, fg, wup, cw, cb, wdn, ng)


def _t5_bucket_matrix():
    q_idx = BLK + np.arange(BLK)
    k_idx = np.arange(2 * BLK)
    dist = q_idx[:, None] - k_idx[None, :]
    max_exact = NUM_BUCKETS // 2
    n = np.maximum(dist, 0)
    large = max_exact + (np.log(np.maximum(n, 1).astype(np.float32) / max_exact)
                         / math.log(T5_MAX_DIST / max_exact)
                         * (NUM_BUCKETS - max_exact)).astype(np.int32)
    large = np.minimum(large, NUM_BUCKETS - 1)
    bucket = np.where(n < max_exact, n, large)
    in_window = (dist >= 0) & (dist < WINDOW)
    return np.where(in_window, bucket, -1).astype(np.int32)


def _row(v):
    return v.reshape(1, -1).astype(F32)


def kernel(x, positions, rel_bias_table, attn_norm_g, w_in, sinks, q_norm_g, w_q_b, kv_norm_g,
           w_kv_b, a_out_norm_g, b_out_norm_g, w_out, ffn_norm_g, w_up, conv_w, conv_b, w_down,
           final_norm_g):
    b, s, d = x.shape
    t = b * s
    assert d == D_MODEL and s % max(PROJ_TM, SWA_ROWS, MLA_TQ, FFN_TM) == 0
    assert attn_norm_g.shape[0] == 1, "one trunk layer"
    l = 0

    zeros64 = jnp.zeros((D_MODEL, LANES - ROPE_DIM), F32)
    kp = w_in[l][:, _C_KPA:_C_KPA + ROPE_DIM]
    k1, k2 = kp[:, :ROPE_HALF], kp[:, ROPE_HALF:]
    w_in_x = jnp.concatenate([w_in[l][:, :_C_KPA], k1, k2, zeros64, k2, k1, zeros64],
                             axis=1).astype(BF16)
    wq = w_q_b[l].reshape(Q_LORA, B_HEADS, NOPE_DIM + ROPE_DIM)
    q1 = wq[:, :, NOPE_DIM:NOPE_DIM + ROPE_HALF]
    q2 = wq[:, :, NOPE_DIM + ROPE_HALF:]
    zq = jnp.zeros((Q_LORA, B_HEADS, LANES - ROPE_DIM), F32)
    wq_x = jnp.concatenate([wq[:, :, :NOPE_DIM], q1, q2, zq, q2, q1, zq], axis=2)
    wq_x = wq_x.reshape(Q_LORA, B_HEADS * _QB_HEAD).astype(BF16)
    wkv = w_kv_b[l].reshape(KV_LORA, B_HEADS, NOPE_DIM + V_DIM)
    wkv_x = jnp.concatenate([wkv[:, :, :NOPE_DIM].reshape(KV_LORA, -1),
                             wkv[:, :, NOPE_DIM:].reshape(KV_LORA, -1)], axis=1).astype(BF16)

    inv_freq = ROPE_THETA ** (-jnp.arange(0, ROPE_DIM, 2, dtype=F32) / ROPE_DIM)
    freq = jnp.tile(inv_freq, LANES // ROPE_HALF).reshape(1, LANES)

    x2 = x.reshape(t, d)
    pos2 = positions.reshape(t, 1).astype(jnp.int32)

    qa, ka, va, qm, km, vb = _proj_call(
        x2, pos2, _row(attn_norm_g[l]), w_in_x, _row(q_norm_g[l]), wq_x,
        _row(kv_norm_g[l]), wkv_x, freq)

    bucket = jnp.asarray(_t5_bucket_matrix())
    out_a = _swa_call(sinks[l].astype(F32), rel_bias_table.astype(F32), bucket,
                      qa.reshape(b, s, -1), ka.reshape(b, s, -1), va.reshape(b, s, -1))
    out_b = _mla_call(qm.reshape(b, s, -1), km.reshape(b, s, -1), vb.reshape(b, s, -1))

    cw = jnp.concatenate([conv_w[l], jnp.zeros((SUBLANES - CONV_W, 2 * D_FF), F32)], axis=0)
    out = _ffn_call(
        x2, out_a.reshape(t, -1), out_b.reshape(t, -1), _row(a_out_norm_g[l]),
        _row(b_out_norm_g[l]), w_out[l].astype(BF16), _row(ffn_norm_g[l]),
        w_up[l].astype(BF16), cw.astype(F32), _row(conv_b[l]), w_down[l].astype(BF16),
        _row(final_norm_g), s // FFN_TM)
    return out.reshape(b, s, d)
```

```python
import functools
import math

import numpy as np
import jax
import jax.numpy as jnp
from jax import lax
from jax.experimental import pallas as pl
from jax.experimental.pallas import tpu as pltpu

F32 = jnp.float32
BF16 = jnp.bfloat16

D_MODEL = 1024
A_HEADS = 8
A_KV_HEADS = 2
A_HEAD_DIM = 64
A_GROUP = A_HEADS // A_KV_HEADS
WINDOW = 128
BLK = WINDOW
A_WIDTH = A_HEADS * A_HEAD_DIM
A_KV_WIDTH = A_KV_HEADS * A_HEAD_DIM
NUM_BUCKETS = 32
T5_MAX_DIST = 128
B_HEADS = 4
Q_LORA = 256
KV_LORA = 128
NOPE_DIM = 128
ROPE_DIM = 64
ROPE_HALF = ROPE_DIM // 2
V_DIM = 128
ROPE_THETA = 10000.0
B_WIDTH = B_HEADS * V_DIM
D_FF = 2816
CONV_W = 3
EPS = 1e-6
NEG = -1e30
LOG2E = math.log2(math.e)

LANES = 128
SUBLANES = 8
QK_PAD = 256
VMEM_LIMIT = 56 * 1024 * 1024

PROJ_TM = 1024
PROJ_TS = 512
SWA_ROWS = 1024
MLA_TQ = 1024
MLA_TK = 1024
MLA_TS = 256
MLA_UNROLL = 8
FFN_TM = 512
FFN_TF = 256
FFN_SLAB = 256
FFN_U_BUFS = 4

_C_QA = 0
_C_KA = _C_QA + A_WIDTH
_C_VA = _C_KA + A_KV_WIDTH
_C_CQ = _C_VA + A_KV_WIDTH
_C_CKV = _C_CQ + Q_LORA
_C_KPA = _C_CKV + KV_LORA
_C_KPB = _C_KPA + LANES
_C_END = _C_KPB + LANES
_QB_HEAD = 3 * LANES


def _rms(x):
    return x * lax.rsqrt(jnp.mean(x * x, axis=-1, keepdims=True) + EPS)


def _proj_kernel(x_ref, pos_ref, g_ref, w_in_ref, qg_ref, wq_ref, kvg_ref, wkv_ref,
                 freq_ref, qa_ref, ka_ref, va_ref, qm_ref, km_ref, vb_ref,
                 cos_ref, sin_ref):
    quarter = PROJ_TM // 4
    lane = lax.broadcasted_iota(jnp.int32, (quarter, LANES), 1)
    pos_q = [pos_ref[j * quarter:(j + 1) * quarter, :].astype(F32) for j in range(4)]
    pos_d = jnp.where(lane < ROPE_HALF, pos_q[0],
                      jnp.where(lane < 2 * ROPE_HALF, pos_q[1],
                                jnp.where(lane < 3 * ROPE_HALF, pos_q[2], pos_q[3])))
    ang = pos_d * freq_ref[...]
    for table, dst_ref, sign in ((jnp.cos(ang), cos_ref, 1.0), (jnp.sin(ang), sin_ref, -1.0)):
        for j in range(4):
            at0 = pltpu.roll(table, (LANES - j * ROPE_HALF) % LANES, 1)
            at1 = pltpu.roll(table, (LANES + ROPE_HALF - j * ROPE_HALF) % LANES, 1)
            dst_ref[j * quarter:(j + 1) * quarter, :] = jnp.where(lane < ROPE_HALF,
                                                                  sign * at0, at1)

    for st in range(PROJ_TM // PROJ_TS):
        r = slice(st * PROJ_TS, (st + 1) * PROJ_TS)
        h = _rms(x_ref[r, :]) * g_ref[...]
        proj = jnp.dot(h.astype(BF16), w_in_ref[...], preferred_element_type=F32)
        qa_ref[r, :] = (proj[:, _C_QA:_C_KA] * (A_HEAD_DIM ** -0.5 * LOG2E)).astype(BF16)
        half = LANES // 2
        for dst_ref, c0 in ((ka_ref, _C_KA), (va_ref, _C_VA)):
            nat = proj[:, c0:c0 + A_KV_WIDTH]
            dst_ref[r, :A_KV_WIDTH] = nat.astype(BF16)
            dst_ref[r, A_KV_WIDTH:] = pltpu.roll(nat, half, 1).astype(BF16)

        cc = cos_ref[r, :]
        ss = sin_ref[r, :]
        kpe = (proj[:, _C_KPA:_C_KPB] * cc + proj[:, _C_KPB:_C_END] * ss).astype(BF16)

        qn = _rms(proj[:, _C_CQ:_C_CKV]) * qg_ref[...]
        qb = jnp.dot(qn.astype(BF16), wq_ref[...], preferred_element_type=F32)
        kvn = _rms(proj[:, _C_CKV:_C_KPA]) * kvg_ref[...]
        kv = jnp.dot(kvn.astype(BF16), wkv_ref[...], preferred_element_type=F32)

        scale = (NOPE_DIM + ROPE_DIM) ** -0.5 * LOG2E
        for hd in range(B_HEADS):
            c0 = hd * _QB_HEAD
            nope = qb[:, c0:c0 + LANES]
            qa_part = qb[:, c0 + LANES:c0 + 2 * LANES]
            qb_part = qb[:, c0 + 2 * LANES:c0 + 3 * LANES]
            o0 = hd * QK_PAD
            qm_ref[r, o0:o0 + LANES] = (nope * scale).astype(BF16)
            qm_ref[r, o0 + LANES:o0 + 2 * LANES] = (
                (qa_part * cc + qb_part * ss) * scale).astype(BF16)
            km_ref[r, o0:o0 + LANES] = kv[:, hd * NOPE_DIM:(hd + 1) * NOPE_DIM].astype(BF16)
            km_ref[r, o0 + LANES:o0 + 2 * LANES] = kpe
        vb_ref[r, :] = kv[:, B_HEADS * NOPE_DIM:].astype(BF16)


def _proj_call(x2, pos2, g, w_in, qg, wq, kvg, wkv, freq):
    t = x2.shape[0]
    tm = PROJ_TM
    row = lambda i: (i, 0)
    fixed = lambda i: (0, 0)
    full = lambda a: pl.BlockSpec(a.shape, fixed)
    out_shape = (
        jax.ShapeDtypeStruct((t, A_WIDTH), BF16),
        jax.ShapeDtypeStruct((t, 2 * A_KV_WIDTH), BF16),
        jax.ShapeDtypeStruct((t, 2 * A_KV_WIDTH), BF16),
        jax.ShapeDtypeStruct((t, B_HEADS * QK_PAD), BF16),
        jax.ShapeDtypeStruct((t, B_HEADS * QK_PAD), BF16),
        jax.ShapeDtypeStruct((t, B_WIDTH), BF16),
    )
    return pl.pallas_call(
        _proj_kernel,
        out_shape=out_shape,
        grid=(t // tm,),
        in_specs=[pl.BlockSpec((tm, D_MODEL), row), pl.BlockSpec((tm, 1), row),
                  full(g), full(w_in), full(qg), full(wq), full(kvg), full(wkv),
                  full(freq)],
        out_specs=tuple(pl.BlockSpec((tm, s.shape[1]), row) for s in out_shape),
        scratch_shapes=[pltpu.VMEM((tm, LANES), F32), pltpu.VMEM((tm, LANES), F32)],
        compiler_params=pltpu.CompilerParams(
            dimension_semantics=("arbitrary",), vmem_limit_bytes=VMEM_LIMIT),
        name="proj",
    )(x2, pos2, g, w_in, qg, wq, kvg, wkv, freq)


def _swa_kernel(sinks_ref, tbl_ref, bucket_ref, q_ref, kp_ref, kc_ref, vp_ref, vc_ref,
                o_ref, bias_ref, p_ref, t_ref):
    step = pl.program_id(1)
    half = LANES // 2

    @pl.when((pl.program_id(0) == 0) & (step == 0))
    def _():
        bk = bucket_ref[...]
        for hd in range(A_HEADS):
            acc = jnp.full((BLK, 2 * BLK), NEG, F32)
            for b in range(NUM_BUCKETS):
                acc = jnp.where(bk == b, tbl_ref[b, hd] * LOG2E, acc)
            bias_ref[hd] = acc

    kidx = lax.broadcasted_iota(jnp.int32, (BLK, 2 * BLK), 1)
    pad_mask = kidx < jnp.where(step == 0, BLK, 0)

    def lane_mask(n_rows, low):
        lane = lax.broadcasted_iota(jnp.int32, (n_rows, LANES), 1)
        return jnp.where((lane < half) == low, 1.0, 0.0).astype(BF16)

    n_keys = BLK + SWA_ROWS
    lo, hi = lane_mask(n_keys, True), lane_mask(n_keys, False)
    ones_low, ones_high = lane_mask(2 * BLK, True), lane_mask(2 * BLK, False)
    lo_q = lax.broadcasted_iota(jnp.int32, (BLK, LANES), 1) < half

    k_all = jnp.concatenate([kp_ref[...], kc_ref[...]], axis=0)
    v_all = jnp.concatenate([vp_ref[...], vc_ref[...]], axis=0)
    k_nat, k_swp = k_all[:, :LANES], k_all[:, LANES:]
    v_nat, v_swp = v_all[:, :LANES], v_all[:, LANES:]
    k_low, k_high = (k_nat * lo, k_swp * lo), (k_swp * hi, k_nat * hi)
    v_low, v_high = (v_nat * lo, v_swp * lo), (v_swp * hi, v_nat * hi)

    dims = (((1,), (1,)), ((), ()))
    n_blk = SWA_ROWS // BLK
    pairs = A_GROUP // 2
    for r in range(n_blk):
        keys = slice(r * BLK, (r + 2) * BLK)
        q_blk = q_ref[r * BLK:(r + 1) * BLK, :]
        for g in range(A_KV_HEADS):
            qg = jnp.concatenate([q_blk[:, (2 * g) * LANES:(2 * g + 1) * LANES],
                                  q_blk[:, (2 * g + 1) * LANES:(2 * g + 2) * LANES]], axis=0)
            s_pair = [lax.dot_general(qg, k_side[g][keys], dims, preferred_element_type=F32)
                      for k_side in (k_low, k_high)]
            for j in range(pairs):
                pair = (r * A_KV_HEADS + g) * pairs + j
                sink_terms = []
                for e in range(2):
                    hd = A_GROUP * g + 2 * j + e
                    s = s_pair[e][j * BLK:(j + 1) * BLK, :]
                    s = s + bias_ref[hd]
                    if r == 0:
                        s = jnp.where(pad_mask, NEG, s)
                    sink = sinks_ref[hd] * LOG2E
                    m = jnp.maximum(jnp.max(s, axis=-1, keepdims=True), sink)
                    p_ref[pair, :, e * 2 * BLK:(e + 1) * 2 * BLK] = jnp.exp2(s - m).astype(BF16)
                    sink_terms.append(jnp.exp2(sink - m))
                t_ref[pair] = jnp.where(lo_q, sink_terms[0], sink_terms[1])

    for r in range(n_blk):
        rows = slice(r * BLK, (r + 1) * BLK)
        keys = slice(r * BLK, (r + 2) * BLK)
        for g in range(A_KV_HEADS):
            rhs = jnp.concatenate(
                [jnp.concatenate([v_low[g][keys], ones_low], axis=1),
                 jnp.concatenate([v_high[g][keys], ones_high], axis=1)], axis=0)
            for j in range(pairs):
                pair = (r * A_KV_HEADS + g) * pairs + j
                res = jnp.dot(p_ref[pair], rhs, preferred_element_type=F32)
                col = (g * pairs + j) * LANES
                o_ref[rows, col:col + LANES] = res[:, :LANES] / (res[:, LANES:] + t_ref[pair])


def _swa_call(sinks, table, bucket, qa, kx, vx):
    b, s, _ = qa.shape
    rows = SWA_ROWS
    per = rows // BLK
    cur = lambda bi, i: (bi, i, 0)
    prev = lambda bi, i: (bi, jnp.maximum(i * per - 1, 0), 0)
    smem = pl.BlockSpec(memory_space=pltpu.SMEM)
    return pl.pallas_call(
        _swa_kernel,
        out_shape=jax.ShapeDtypeStruct((b, s, A_WIDTH), F32),
        grid=(b, s // rows),
        in_specs=[smem, smem, pl.BlockSpec(bucket.shape, lambda bi, i: (0, 0)),
                  pl.BlockSpec((None, rows, A_WIDTH), cur),
                  pl.BlockSpec((None, BLK, 2 * A_KV_WIDTH), prev),
                  pl.BlockSpec((None, rows, 2 * A_KV_WIDTH), cur),
                  pl.BlockSpec((None, BLK, 2 * A_KV_WIDTH), prev),
                  pl.BlockSpec((None, rows, 2 * A_KV_WIDTH), cur)],
        out_specs=pl.BlockSpec((None, rows, A_WIDTH), cur),
        scratch_shapes=[pltpu.VMEM((A_HEADS, BLK, 2 * BLK), F32),
                        pltpu.VMEM((per * A_HEADS // 2, BLK, 4 * BLK), BF16),
                        pltpu.VMEM((per * A_HEADS // 2, BLK, LANES), F32)],
        compiler_params=pltpu.CompilerParams(
            dimension_semantics=("arbitrary", "arbitrary"), vmem_limit_bytes=VMEM_LIMIT),
        name="swa",
    )(sinks, table, bucket, qa, kx, kx, vx, vx)


def _mla_kernel(q_ref, qn_ref, k_ref, v_ref, o_ref,
                s0_ref, s1_ref, s2_ref, x0_ref, x1_ref, x2_ref, m_ref, acc_ref):
    qi = pl.program_id(2)
    n_q = pl.num_programs(2)
    tq, tk, ts = MLA_TQ, MLA_TK, MLA_TS
    streams = tq // ts
    buf0 = (s0_ref, x0_ref)
    buf1 = (s1_ref, x1_ref)
    buf2 = (s2_ref, x2_ref)

    def scores(c, dst, queries=q_ref):
        s_ref, x_ref = dst
        start = pl.multiple_of(c * tk, tk)
        k = k_ref[pl.ds(start, tk), :]
        for h in range(streams):
            s = lax.dot_general(queries[h * ts:(h + 1) * ts, :], k,
                                (((1,), (1,)), ((), ())), preferred_element_type=F32)
            s_ref[h] = s
            blk = s[:, 0:LANES]
            for j in range(1, tk // LANES):
                blk = jnp.maximum(blk, s[:, j * LANES:(j + 1) * LANES])
            x_ref[h] = blk

    def accumulate(c, src, diagonal):
        s_ref, x_ref = src
        start = pl.multiple_of(c * tk, tk)
        for h in range(streams):
            nk = (h + 1) * ts if diagonal else tk
            v = v_ref[pl.ds(start, nk), :]
            v1 = jnp.concatenate([v, jnp.ones_like(v)], axis=1)
            s = s_ref[h, :, 0:nk]
            if diagonal:
                qrow = h * ts + lax.broadcasted_iota(jnp.int32, (ts, nk), 0)
                kcol = lax.broadcasted_iota(jnp.int32, (ts, nk), 1)
                s = jnp.where(qrow >= kcol, s, NEG)
                row_max = jnp.max(s, axis=-1, keepdims=True)
            else:
                row_max = jnp.max(x_ref[h], axis=-1, keepdims=True)
            m = m_ref[h]
            m_new = jnp.maximum(m, row_max)
            alpha = jnp.exp2(m - m_new)
            p = jnp.exp2(s - m_new).astype(BF16)
            acc_ref[h] = alpha * acc_ref[h] + jnp.dot(p, v1, preferred_element_type=F32)
            m_ref[h] = m_new

    def tick(c, src, dst):
        scores(c + 1, dst)
        accumulate(c, src, False)

    m_ref[...] = jnp.full(m_ref.shape, NEG, F32)
    acc_ref[...] = jnp.zeros(acc_ref.shape, F32)

    odd = (qi % 2) == 1
    even_later = (qi > 0) & jnp.logical_not(odd)

    @pl.when(qi == 0)
    def _():
        scores(0, buf0)

    @pl.when(odd)
    def _():
        tick(0, buf2, buf0)

    @pl.when(even_later)
    def _():
        tick(0, buf2, buf1)
        tick(1, buf1, buf0)

    def ticks(c, count):
        for i in range(0, count, 2):
            tick(c + i, buf0, buf1)
            tick(c + i + 1, buf1, buf0)

    done = jnp.where(qi == 0, 0, jnp.where(odd, 1, 2))
    rest = qi - done
    size = 2
    while size < MLA_UNROLL:
        pl.when((rest & size) != 0)(functools.partial(ticks, done, size))
        done = done + (rest & size)
        size *= 2

    def group(t, carry):
        ticks(done + MLA_UNROLL * t, MLA_UNROLL)
        return carry

    lax.fori_loop(0, rest // MLA_UNROLL, group, 0)

    @pl.when(qi + 1 < n_q)
    def _():
        scores(0, buf2, qn_ref)
        accumulate(qi, buf0, True)

    @pl.when(qi + 1 == n_q)
    def _():
        accumulate(qi, buf0, True)

    for h in range(streams):
        acc = acc_ref[h]
        o_ref[h * ts:(h + 1) * ts, :] = acc[:, :V_DIM] / acc[:, V_DIM:]


def _mla_call(qm, km, vb):
    b, s, _ = qm.shape
    tq, tk, ts = MLA_TQ, MLA_TK, MLA_TS
    assert tq == tk and tq % ts == 0
    streams = tq // ts
    n_q = s // tq
    score_buf = pltpu.VMEM((streams, ts, tk), F32)
    max_buf = pltpu.VMEM((streams, ts, LANES), F32)
    return pl.pallas_call(
        _mla_kernel,
        out_shape=jax.ShapeDtypeStruct((b, s, B_WIDTH), F32),
        grid=(b, B_HEADS, n_q),
        in_specs=[pl.BlockSpec((None, tq, QK_PAD), lambda bi, h, i: (bi, i, h)),
                  pl.BlockSpec((None, tq, QK_PAD),
                               lambda bi, h, i: (bi, jnp.minimum(i + 1, n_q - 1), h)),
                  pl.BlockSpec((None, s, QK_PAD), lambda bi, h, i: (bi, 0, h)),
                  pl.BlockSpec((None, s, V_DIM), lambda bi, h, i: (bi, 0, h))],
        out_specs=pl.BlockSpec((None, tq, V_DIM), lambda bi, h, i: (bi, i, h)),
        scratch_shapes=[score_buf, score_buf, score_buf, max_buf, max_buf, max_buf,
                        pltpu.VMEM((streams, ts, 1), F32),
                        pltpu.VMEM((streams, ts, 2 * V_DIM), F32)],
        compiler_params=pltpu.CompilerParams(
            dimension_semantics=("arbitrary", "arbitrary", "arbitrary"),
            vmem_limit_bytes=VMEM_LIMIT),
        name="mla",
    )(qm, qm, km, vb)


def _ffn_kernel(tiles_per_seq, x_ref, oa_ref, ob_ref, ag_ref, bg_ref, wout_ref, fg_ref,
                wup_ref, cw_ref, cb_ref, wdn_ref, ng_ref, o_ref,
                carry_ref, u_ref, act0_ref, act1_ref, xa_ref, xb_ref):
    tm, tf = FFN_TM, FFN_TF
    step = pl.program_id(0)
    n_chunks = D_FF // tf
    n_slabs = D_MODEL // FFN_SLAB
    slab_after = {int((q + 0.5) * n_chunks / n_slabs): (q,) for q in range(n_slabs)}
    assert len(slab_after) == n_slabs

    @pl.when(step == 0)
    def _():
        act1_ref[...] = jnp.zeros_like(act1_ref)
        xb_ref[...] = jnp.zeros_like(xb_ref)

    @pl.when((step % tiles_per_seq) == 0)
    def _():
        carry_ref[...] = jnp.zeros_like(carry_ref)

    def conv(u_ref, u, cols):
        u_ref[0:SUBLANES, :] = carry_ref[:, cols]
        u_ref[SUBLANES:SUBLANES + tm, :] = u
        carry_ref[:, cols] = u_ref[tm:tm + SUBLANES, :]
        w = cw_ref[:, cols]
        y = (u_ref[SUBLANES - 2:SUBLANES - 2 + tm, :] * w[0:1]
             + u_ref[SUBLANES - 1:SUBLANES - 1 + tm, :] * w[1:2]
             + u * w[2:3])
        return y + cb_ref[:, cols]

    def body(act_w, x_w, act_r, x_r):
        mixed = jnp.concatenate([_rms(oa_ref[...]) * ag_ref[...],
                                 _rms(ob_ref[...]) * bg_ref[...]], axis=-1)
        x1 = x_ref[...] + jnp.dot(mixed.astype(BF16), wout_ref[...],
                                  preferred_element_type=F32)
        x_w[...] = x1
        h2 = (_rms(x1) * fg_ref[...]).astype(BF16)

        for j in range(n_chunks):
            gcols = slice(j * tf, (j + 1) * tf)
            vcols = slice(D_FF + j * tf, D_FF + (j + 1) * tf)
            ug = jnp.dot(h2, wup_ref[:, gcols], preferred_element_type=F32)
            uv = jnp.dot(h2, wup_ref[:, vcols], preferred_element_type=F32)
            gate = conv(u_ref.at[(2 * j) % FFN_U_BUFS], ug, gcols)
            val = conv(u_ref.at[(2 * j + 1) % FFN_U_BUFS], uv, vcols)
            act_w[:, gcols] = (gate * (1.0 / (1.0 + jnp.exp(-gate))) * val).astype(BF16)
            for q in slab_after.get(j, ()):
                cols = slice(q * FFN_SLAB, (q + 1) * FFN_SLAB)
                o_ref[:, cols] = x_r[:, cols] + jnp.dot(act_r[...], wdn_ref[:, cols],
                                                        preferred_element_type=F32)
        o_ref[...] = _rms(o_ref[...]) * ng_ref[...]

    pl.when(step % 2 == 0)(functools.partial(body, act0_ref, xa_ref, act1_ref, xb_ref))
    pl.when(step % 2 == 1)(functools.partial(body, act1_ref, xb_ref, act0_ref, xa_ref))


def _ffn_call(x2, oa, ob, ag, bg, wout, fg, wup, cw, cb, wdn, ng, tiles_per_seq):
    t = x2.shape[0]
    tm = FFN_TM
    n_tiles = t // tm
    row = lambda i: (jnp.minimum(i, n_tiles - 1), 0)
    out_row = lambda i: (jnp.maximum(i - 1, 0), 0)
    fixed = lambda i: (0, 0)
    full = lambda a: pl.BlockSpec(a.shape, fixed, pipeline_mode=pl.Buffered(1))
    return pl.pallas_call(
        functools.partial(_ffn_kernel, tiles_per_seq),
        out_shape=jax.ShapeDtypeStruct((t, D_MODEL), F32),
        grid=(n_tiles + 1,),
        in_specs=[pl.BlockSpec((tm, D_MODEL), row), pl.BlockSpec((tm, A_WIDTH), row),
                  pl.BlockSpec((tm, B_WIDTH), row), full(ag), full(bg), full(wout), full(fg),
                  full(wup), full(cw), full(cb), full(wdn), full(ng)],
        out_specs=pl.BlockSpec((tm, D_MODEL), out_row),
        scratch_shapes=[pltpu.VMEM((SUBLANES, 2 * D_FF), F32),
                        pltpu.VMEM((FFN_U_BUFS, tm + SUBLANES, FFN_TF), F32),
                        pltpu.VMEM((tm, D_FF), BF16),
                        pltpu.VMEM((tm, D_FF), BF16),
                        pltpu.VMEM((tm, D_MODEL), F32),
                        pltpu.VMEM((tm, D_MODEL), F32)],
        compiler_params=pltpu.CompilerParams(
            dimension_semantics=("arbitrary",), vmem_limit_bytes=VMEM_LIMIT),
        name="ffn",
    )(x2, oa, ob, ag, bg, wout, fg, wup, cw, cb, wdn, ng)


def _t5_bucket_matrix():
    q_idx = BLK + np.arange(BLK)
    k_idx = np.arange(2 * BLK)
    dist = q_idx[:, None] - k_idx[None, :]
    max_exact = NUM_BUCKETS // 2
    n = np.maximum(dist, 0)
    large = max_exact + (np.log(np.maximum(n, 1).astype(np.float32) / max_exact)
                         / math.log(T5_MAX_DIST / max_exact)
                         * (NUM_BUCKETS - max_exact)).astype(np.int32)
    large = np.minimum(large, NUM_BUCKETS - 1)
    bucket = np.where(n < max_exact, n, large)
    in_window = (dist >= 0) & (dist < WINDOW)
    return np.where(in_window, bucket, -1).astype(np.int32)


def _row(v):
    return v.reshape(1, -1).astype(F32)


def kernel(x, positions, rel_bias_table, attn_norm_g, w_in, sinks, q_norm_g, w_q_b, kv_norm_g,
           w_kv_b, a_out_norm_g, b_out_norm_g, w_out, ffn_norm_g, w_up, conv_w, conv_b, w_down,
           final_norm_g):
    b, s, d = x.shape
    t = b * s
    assert d == D_MODEL and s % max(PROJ_TM, SWA_ROWS, MLA_TQ, FFN_TM) == 0
    assert attn_norm_g.shape[0] == 1, "one trunk layer"
    l = 0

    zeros64 = jnp.zeros((D_MODEL, LANES - ROPE_DIM), F32)
    kp = w_in[l][:, _C_KPA:_C_KPA + ROPE_DIM]
    k1, k2 = kp[:, :ROPE_HALF], kp[:, ROPE_HALF:]
    w_in_x = jnp.concatenate([w_in[l][:, :_C_KPA], k1, k2, zeros64, k2, k1, zeros64],
                             axis=1).astype(BF16)
    wq = w_q_b[l].reshape(Q_LORA, B_HEADS, NOPE_DIM + ROPE_DIM)
    q1 = wq[:, :, NOPE_DIM:NOPE_DIM + ROPE_HALF]
    q2 = wq[:, :, NOPE_DIM + ROPE_HALF:]
    zq = jnp.zeros((Q_LORA, B_HEADS, LANES - ROPE_DIM), F32)
    wq_x = jnp.concatenate([wq[:, :, :NOPE_DIM], q1, q2, zq, q2, q1, zq], axis=2)
    wq_x = wq_x.reshape(Q_LORA, B_HEADS * _QB_HEAD).astype(BF16)
    wkv = w_kv_b[l].reshape(KV_LORA, B_HEADS, NOPE_DIM + V_DIM)
    wkv_x = jnp.concatenate([wkv[:, :, :NOPE_DIM].reshape(KV_LORA, -1),
                             wkv[:, :, NOPE_DIM:].reshape(KV_LORA, -1)], axis=1).astype(BF16)

    inv_freq = ROPE_THETA ** (-jnp.arange(0, ROPE_DIM, 2, dtype=F32) / ROPE_DIM)
    freq = jnp.tile(inv_freq, LANES // ROPE_HALF).reshape(1, LANES)

    x2 = x.reshape(t, d)
    pos2 = positions.reshape(t, 1).astype(jnp.int32)

    qa, ka, va, qm, km, vb = _proj_call(
        x2, pos2, _row(attn_norm_g[l]), w_in_x, _row(q_norm_g[l]), wq_x,
        _row(kv_norm_g[l]), wkv_x, freq)

    bucket = jnp.asarray(_t5_bucket_matrix())
    out_a = _swa_call(sinks[l].astype(F32), rel_bias_table.astype(F32), bucket,
                      qa.reshape(b, s, -1), ka.reshape(b, s, -1), va.reshape(b, s, -1))
    out_b = _mla_call(qm.reshape(b, s, -1), km.reshape(b, s, -1), vb.reshape(b, s, -1))

    cw = jnp.concatenate([conv_w[l], jnp.zeros((SUBLANES - CONV_W, 2 * D_FF), F32)], axis=0)
    out = _ffn_call(
        x2, out_a.reshape(t, -1), out_b.reshape(t, -1), _row(a_out_norm_g[l]),
        _row(b_out_norm_g[l]), w_out[l].astype(BF16), _row(ffn_norm_g[l]),
        w_up[l].astype(BF16), cw.astype(F32), _row(conv_b[l]), w_down[l].astype(BF16),
        _row(final_norm_g), s // FFN_TM)
    return out.reshape(b, s, d)
```

```python
import functools
import math

import numpy as np
import jax
import jax.numpy as jnp
from jax import lax
from jax.experimental import pallas as pl
from jax.experimental.pallas import tpu as pltpu

F32 = jnp.float32
BF16 = jnp.bfloat16

D_MODEL = 1024
A_HEADS = 8
A_KV_HEADS = 2
A_HEAD_DIM = 64
A_GROUP = A_HEADS // A_KV_HEADS
WINDOW = 128
BLK = WINDOW
A_WIDTH = A_HEADS * A_HEAD_DIM
A_KV_WIDTH = A_KV_HEADS * A_HEAD_DIM
NUM_BUCKETS = 32
T5_MAX_DIST = 128
B_HEADS = 4
Q_LORA = 256
KV_LORA = 128
NOPE_DIM = 128
ROPE_DIM = 64
ROPE_HALF = ROPE_DIM // 2
V_DIM = 128
ROPE_THETA = 10000.0
B_WIDTH = B_HEADS * V_DIM
D_FF = 2816
CONV_W = 3
EPS = 1e-6
NEG = -1e30
LOG2E = math.log2(math.e)

LANES = 128
SUBLANES = 8
QK_PAD = 256
VMEM_LIMIT = 56 * 1024 * 1024

PROJ_TM = 1024
PROJ_TS = 512
SWA_ROWS = 1024
MLA_TQ = 1024
MLA_TK = 1024
MLA_TS = 256
MLA_UNROLL = 8
FFN_TM = 512
FFN_TF = 256
FFN_SLAB = 256
FFN_U_BUFS = 4

_C_QA = 0
_C_KA = _C_QA + A_WIDTH
_C_VA = _C_KA + A_KV_WIDTH
_C_CQ = _C_VA + A_KV_WIDTH
_C_CKV = _C_CQ + Q_LORA
_C_KPA = _C_CKV + KV_LORA
_C_END = _C_KPA + LANES
_QB_HEAD = 2 * LANES


def _rms(x):
    return x * lax.rsqrt(jnp.mean(x * x, axis=-1, keepdims=True) + EPS)


def _proj_kernel(x_ref, pos_ref, g_ref, w_in_ref, qg_ref, wq_ref, kvg_ref, wkv_ref,
                 freq_ref, qa_ref, ka_ref, va_ref, qm_ref, km_ref, vb_ref,
                 cos_ref, sin_ref):
    quarter = PROJ_TM // 4
    lane = lax.broadcasted_iota(jnp.int32, (quarter, LANES), 1)
    pos_q = [pos_ref[j * quarter:(j + 1) * quarter, :].astype(F32) for j in range(4)]
    pos_d = jnp.where(lane < ROPE_HALF, pos_q[0],
                      jnp.where(lane < 2 * ROPE_HALF, pos_q[1],
                                jnp.where(lane < 3 * ROPE_HALF, pos_q[2], pos_q[3])))
    ang = pos_d * freq_ref[...]
    for table, dst_ref, sign in ((jnp.cos(ang), cos_ref, 1.0), (jnp.sin(ang), sin_ref, -1.0)):
        for j in range(4):
            at0 = pltpu.roll(table, (LANES - j * ROPE_HALF) % LANES, 1)
            at1 = pltpu.roll(table, (LANES + ROPE_HALF - j * ROPE_HALF) % LANES, 1)
            spread = jnp.where(lane < ROPE_HALF, sign * at0, at1)
            if dst_ref is sin_ref:
                spread = jnp.where(lane < ROPE_DIM, spread, 0.0)
            dst_ref[j * quarter:(j + 1) * quarter, :] = spread

    lane_s = lax.broadcasted_iota(jnp.int32, (PROJ_TS, LANES), 1)

    def swap(x):
        return jnp.where(lane_s < ROPE_HALF, pltpu.roll(x, LANES - ROPE_HALF, 1),
                         pltpu.roll(x, ROPE_HALF, 1))

    n_st = PROJ_TM // PROJ_TS
    rows = [slice(st * PROJ_TS, (st + 1) * PROJ_TS) for st in range(n_st)]
    half = LANES // 2
    projs = []
    for r in rows:
        h = _rms(x_ref[r, :]) * g_ref[...]
        projs.append(jnp.dot(h.astype(BF16), w_in_ref[...], preferred_element_type=F32))

    ups = []
    for r, proj in zip(rows, projs):
        qa_ref[r, :] = (proj[:, _C_QA:_C_KA] * (A_HEAD_DIM ** -0.5 * LOG2E)).astype(BF16)
        for dst_ref, c0 in ((ka_ref, _C_KA), (va_ref, _C_VA)):
            nat = proj[:, c0:c0 + A_KV_WIDTH]
            dst_ref[r, :A_KV_WIDTH] = nat.astype(BF16)
            dst_ref[r, A_KV_WIDTH:] = pltpu.roll(nat, half, 1).astype(BF16)
        qn = _rms(proj[:, _C_CQ:_C_CKV]) * qg_ref[...]
        qb = jnp.dot(qn.astype(BF16), wq_ref[...], preferred_element_type=F32)
        kvn = _rms(proj[:, _C_CKV:_C_KPA]) * kvg_ref[...]
        kv = jnp.dot(kvn.astype(BF16), wkv_ref[...], preferred_element_type=F32)
        ups.append((qb, kv))

    scale = (NOPE_DIM + ROPE_DIM) ** -0.5 * LOG2E
    for r, proj, (qb, kv) in zip(rows, projs, ups):
        cc = cos_ref[r, :]
        ss = sin_ref[r, :]
        k_rot = proj[:, _C_KPA:_C_END]
        kpe = (k_rot * cc + swap(k_rot) * ss).astype(BF16)
        for hd in range(B_HEADS):
            c0 = hd * _QB_HEAD
            nope = qb[:, c0:c0 + LANES]
            q_rot = qb[:, c0 + LANES:c0 + 2 * LANES]
            o0 = hd * QK_PAD
            qm_ref[r, o0:o0 + LANES] = (nope * scale).astype(BF16)
            qm_ref[r, o0 + LANES:o0 + 2 * LANES] = (
                (q_rot * cc + swap(q_rot) * ss) * scale).astype(BF16)
            km_ref[r, o0:o0 + LANES] = kv[:, hd * NOPE_DIM:(hd + 1) * NOPE_DIM].astype(BF16)
            km_ref[r, o0 + LANES:o0 + 2 * LANES] = kpe
        vb_ref[r, :] = kv[:, B_HEADS * NOPE_DIM:].astype(BF16)


def _proj_call(x2, pos2, g, w_in, qg, wq, kvg, wkv, freq):
    t = x2.shape[0]
    tm = PROJ_TM
    row = lambda i: (i, 0)
    fixed = lambda i: (0, 0)
    full = lambda a: pl.BlockSpec(a.shape, fixed)
    out_shape = (
        jax.ShapeDtypeStruct((t, A_WIDTH), BF16),
        jax.ShapeDtypeStruct((t, 2 * A_KV_WIDTH), BF16),
        jax.ShapeDtypeStruct((t, 2 * A_KV_WIDTH), BF16),
        jax.ShapeDtypeStruct((t, B_HEADS * QK_PAD), BF16),
        jax.ShapeDtypeStruct((t, B_HEADS * QK_PAD), BF16),
        jax.ShapeDtypeStruct((t, B_WIDTH), BF16),
    )
    return pl.pallas_call(
        _proj_kernel,
        out_shape=out_shape,
        grid=(t // tm,),
        in_specs=[pl.BlockSpec((tm, D_MODEL), row), pl.BlockSpec((tm, 1), row),
                  full(g), full(w_in), full(qg), full(wq), full(kvg), full(wkv),
                  full(freq)],
        out_specs=tuple(pl.BlockSpec((tm, s.shape[1]), row) for s in out_shape),
        scratch_shapes=[pltpu.VMEM((tm, LANES), F32), pltpu.VMEM((tm, LANES), F32)],
        compiler_params=pltpu.CompilerParams(
            dimension_semantics=("arbitrary",), vmem_limit_bytes=VMEM_LIMIT),
        name="proj",
    )(x2, pos2, g, w_in, qg, wq, kvg, wkv, freq)


def _swa_kernel(sinks_ref, tbl_ref, bucket_ref, q_ref, kp_ref, kc_ref, vp_ref, vc_ref,
                o_ref, bias_ref, p_ref, t_ref):
    step = pl.program_id(1)
    half = LANES // 2

    @pl.when((pl.program_id(0) == 0) & (step == 0))
    def _():
        bk = bucket_ref[...]
        for hd in range(A_HEADS):
            acc = jnp.full((BLK, 2 * BLK), NEG, F32)
            for b in range(NUM_BUCKETS):
                acc = jnp.where(bk == b, tbl_ref[b, hd] * LOG2E, acc)
            bias_ref[hd] = acc

    kidx = lax.broadcasted_iota(jnp.int32, (BLK, 2 * BLK), 1)
    pad_mask = kidx < jnp.where(step == 0, BLK, 0)

    def lane_mask(n_rows, low):
        lane = lax.broadcasted_iota(jnp.int32, (n_rows, LANES), 1)
        return jnp.where((lane < half) == low, 1.0, 0.0).astype(BF16)

    n_keys = BLK + SWA_ROWS
    lo, hi = lane_mask(n_keys, True), lane_mask(n_keys, False)
    ones_low, ones_high = lane_mask(2 * BLK, True), lane_mask(2 * BLK, False)
    lo_q = lax.broadcasted_iota(jnp.int32, (BLK, LANES), 1) < half

    k_all = jnp.concatenate([kp_ref[...], kc_ref[...]], axis=0)
    v_all = jnp.concatenate([vp_ref[...], vc_ref[...]], axis=0)
    k_nat, k_swp = k_all[:, :LANES], k_all[:, LANES:]
    v_nat, v_swp = v_all[:, :LANES], v_all[:, LANES:]
    k_low, k_high = (k_nat * lo, k_swp * lo), (k_swp * hi, k_nat * hi)
    v_low, v_high = (v_nat * lo, v_swp * lo), (v_swp * hi, v_nat * hi)

    dims = (((1,), (1,)), ((), ()))
    n_blk = SWA_ROWS // BLK
    pairs = A_GROUP // 2

    def pass1(r):
        keys = slice(r * BLK, (r + 2) * BLK)
        q_blk = q_ref[r * BLK:(r + 1) * BLK, :]
        for g in range(A_KV_HEADS):
            qg = jnp.concatenate([q_blk[:, (2 * g) * LANES:(2 * g + 1) * LANES],
                                  q_blk[:, (2 * g + 1) * LANES:(2 * g + 2) * LANES]], axis=0)
            s_pair = [lax.dot_general(qg, k_side[g][keys], dims, preferred_element_type=F32)
                      for k_side in (k_low, k_high)]
            for j in range(pairs):
                pair = (r * A_KV_HEADS + g) * pairs + j
                sink_terms = []
                for e in range(2):
                    hd = A_GROUP * g + 2 * j + e
                    s = s_pair[e][j * BLK:(j + 1) * BLK, :]
                    s = s + bias_ref[hd]
                    if r == 0:
                        s = jnp.where(pad_mask, NEG, s)
                    sink = sinks_ref[hd] * LOG2E
                    m = jnp.maximum(jnp.max(s, axis=-1, keepdims=True), sink)
                    p_ref[pair, :, e * 2 * BLK:(e + 1) * 2 * BLK] = jnp.exp2(s - m).astype(BF16)
                    sink_terms.append(jnp.exp2(sink - m))
                t_ref[pair] = jnp.where(lo_q, sink_terms[0], sink_terms[1])

    def pass2(r):
        rows = slice(r * BLK, (r + 1) * BLK)
        keys = slice(r * BLK, (r + 2) * BLK)
        for g in range(A_KV_HEADS):
            rhs = jnp.concatenate(
                [jnp.concatenate([v_low[g][keys], ones_low], axis=1),
                 jnp.concatenate([v_high[g][keys], ones_high], axis=1)], axis=0)
            for j in range(pairs):
                pair = (r * A_KV_HEADS + g) * pairs + j
                res = jnp.dot(p_ref[pair], rhs, preferred_element_type=F32)
                col = (g * pairs + j) * LANES
                o_ref[rows, col:col + LANES] = res[:, :LANES] / (res[:, LANES:] + t_ref[pair])

    for r in range(n_blk):
        pass1(r)
    for r in range(n_blk):
        pass2(r)


def _swa_call(sinks, table, bucket, qa, kx, vx):
    b, s, _ = qa.shape
    rows = SWA_ROWS
    per = rows // BLK
    cur = lambda bi, i: (bi, i, 0)
    prev = lambda bi, i: (bi, jnp.maximum(i * per - 1, 0), 0)
    smem = pl.BlockSpec(memory_space=pltpu.SMEM)
    return pl.pallas_call(
        _swa_kernel,
        out_shape=jax.ShapeDtypeStruct((b, s, A_WIDTH), F32),
        grid=(b, s // rows),
        in_specs=[smem, smem, pl.BlockSpec(bucket.shape, lambda bi, i: (0, 0)),
                  pl.BlockSpec((None, rows, A_WIDTH), cur),
                  pl.BlockSpec((None, BLK, 2 * A_KV_WIDTH), prev),
                  pl.BlockSpec((None, rows, 2 * A_KV_WIDTH), cur),
                  pl.BlockSpec((None, BLK, 2 * A_KV_WIDTH), prev),
                  pl.BlockSpec((None, rows, 2 * A_KV_WIDTH), cur)],
        out_specs=pl.BlockSpec((None, rows, A_WIDTH), cur),
        scratch_shapes=[pltpu.VMEM((A_HEADS, BLK, 2 * BLK), F32),
                        pltpu.VMEM((per * A_HEADS // 2, BLK, 4 * BLK), BF16),
                        pltpu.VMEM((per * A_HEADS // 2, BLK, LANES), F32)],
        compiler_params=pltpu.CompilerParams(
            dimension_semantics=("arbitrary", "arbitrary"), vmem_limit_bytes=VMEM_LIMIT),
        name="swa",
    )(sinks, table, bucket, qa, kx, kx, vx, vx)


def _mla_kernel(q_ref, qn_ref, k_ref, v_ref, o_ref,
                s0_ref, s1_ref, s2_ref, x0_ref, x1_ref, x2_ref, m_ref, acc_ref):
    qi = pl.program_id(2)
    n_q = pl.num_programs(2)
    tq, tk, ts = MLA_TQ, MLA_TK, MLA_TS
    streams = tq // ts
    buf0 = (s0_ref, x0_ref)
    buf1 = (s1_ref, x1_ref)
    buf2 = (s2_ref, x2_ref)

    def scores(c, dst, queries=q_ref):
        s_ref, x_ref = dst
        start = pl.multiple_of(c * tk, tk)
        k = k_ref[pl.ds(start, tk), :]
        for h in range(streams):
            s = lax.dot_general(queries[h * ts:(h + 1) * ts, :], k,
                                (((1,), (1,)), ((), ())), preferred_element_type=F32)
            s_ref[h] = s
            blk = s[:, 0:LANES]
            for j in range(1, tk // LANES):
                blk = jnp.maximum(blk, s[:, j * LANES:(j + 1) * LANES])
            x_ref[h] = blk

    def accumulate(c, src, diagonal):
        s_ref, x_ref = src
        start = pl.multiple_of(c * tk, tk)
        for h in range(streams):
            nk = (h + 1) * ts if diagonal else tk
            v = v_ref[pl.ds(start, nk), :]
            v1 = jnp.concatenate([v, jnp.ones_like(v)], axis=1)
            s = s_ref[h, :, 0:nk]
            if diagonal:
                qrow = h * ts + lax.broadcasted_iota(jnp.int32, (ts, nk), 0)
                kcol = lax.broadcasted_iota(jnp.int32, (ts, nk), 1)
                s = jnp.where(qrow >= kcol, s, NEG)
                row_max = jnp.max(s, axis=-1, keepdims=True)
            else:
                row_max = jnp.max(x_ref[h], axis=-1, keepdims=True)
            m = m_ref[h]
            m_new = jnp.maximum(m, row_max)
            alpha = jnp.exp2(m - m_new)
            p = jnp.exp2(s - m_new).astype(BF16)
            acc_ref[h] = alpha * acc_ref[h] + jnp.dot(p, v1, preferred_element_type=F32)
            m_ref[h] = m_new

    def tick(c, src, dst):
        scores(c + 1, dst)
        accumulate(c, src, False)

    m_ref[...] = jnp.full(m_ref.shape, NEG, F32)
    acc_ref[...] = jnp.zeros(acc_ref.shape, F32)

    odd = (qi % 2) == 1
    even_later = (qi > 0) & jnp.logical_not(odd)

    @pl.when(qi == 0)
    def _():
        scores(0, buf0)

    @pl.when(odd)
    def _():
        tick(0, buf2, buf0)

    @pl.when(even_later)
    def _():
        tick(0, buf2, buf1)
        tick(1, buf1, buf0)

    def ticks(c, count):
        for i in range(0, count, 2):
            tick(c + i, buf0, buf1)
            tick(c + i + 1, buf1, buf0)

    done = jnp.where(qi == 0, 0, jnp.where(odd, 1, 2))
    rest = qi - done
    size = 2
    while size < MLA_UNROLL:
        pl.when((rest & size) != 0)(functools.partial(ticks, done, size))
        done = done + (rest & size)
        size *= 2

    def group(t, carry):
        ticks(done + MLA_UNROLL * t, MLA_UNROLL)
        return carry

    lax.fori_loop(0, rest // MLA_UNROLL, group, 0)

    @pl.when(qi + 1 < n_q)
    def _():
        scores(0, buf2, qn_ref)
        accumulate(qi, buf0, True)

    @pl.when(qi + 1 == n_q)
    def _():
        accumulate(qi, buf0, True)

    for h in range(streams):
        acc = acc_ref[h]
        o_ref[h * ts:(h + 1) * ts, :] = acc[:, :V_DIM] / acc[:, V_DIM:]


def _mla_call(qm, km, vb):
    b, s, _ = qm.shape
    tq, tk, ts = MLA_TQ, MLA_TK, MLA_TS
    assert tq == tk and tq % ts == 0
    streams = tq // ts
    n_q = s // tq
    score_buf = pltpu.VMEM((streams, ts, tk), F32)
    max_buf = pltpu.VMEM((streams, ts, LANES), F32)
    return pl.pallas_call(
        _mla_kernel,
        out_shape=jax.ShapeDtypeStruct((b, s, B_WIDTH), F32),
        grid=(b, B_HEADS, n_q),
        in_specs=[pl.BlockSpec((None, tq, QK_PAD), lambda bi, h, i: (bi, i, h)),
                  pl.BlockSpec((None, tq, QK_PAD),
                               lambda bi, h, i: (bi, jnp.minimum(i + 1, n_q - 1), h)),
                  pl.BlockSpec((None, s, QK_PAD), lambda bi, h, i: (bi, 0, h)),
                  pl.BlockSpec((None, s, V_DIM), lambda bi, h, i: (bi, 0, h))],
        out_specs=pl.BlockSpec((None, tq, V_DIM), lambda bi, h, i: (bi, i, h)),
        scratch_shapes=[score_buf, score_buf, score_buf, max_buf, max_buf, max_buf,
                        pltpu.VMEM((streams, ts, 1), F32),
                        pltpu.VMEM((streams, ts, 2 * V_DIM), F32)],
        compiler_params=pltpu.CompilerParams(
            dimension_semantics=("arbitrary", "arbitrary", "arbitrary"),
            vmem_limit_bytes=VMEM_LIMIT),
        name="mla",
    )(qm, qm, km, vb)


def _ffn_kernel(tiles_per_seq, x_ref, oa_ref, ob_ref, ag_ref, bg_ref, wout_ref, fg_ref,
                wup_ref, cw_ref, cb_ref, wdn_ref, ng_ref, o_ref,
                carry_ref, u_ref, act0_ref, act1_ref, xa_ref, xb_ref):
    tm, tf = FFN_TM, FFN_TF
    step = pl.program_id(0)
    n_chunks = D_FF // tf
    n_slabs = D_MODEL // FFN_SLAB
    slab_after = {int((q + 0.5) * n_chunks / n_slabs): (q,) for q in range(n_slabs)}
    assert len(slab_after) == n_slabs

    @pl.when(step == 0)
    def _():
        act1_ref[...] = jnp.zeros_like(act1_ref)
        xb_ref[...] = jnp.zeros_like(xb_ref)

    @pl.when((step % tiles_per_seq) == 0)
    def _():
        carry_ref[...] = jnp.zeros_like(carry_ref)

    def conv(u_ref, u, cols):
        u_ref[0:SUBLANES, :] = carry_ref[:, cols]
        u_ref[SUBLANES:SUBLANES + tm, :] = u
        carry_ref[:, cols] = u_ref[tm:tm + SUBLANES, :]
        w = cw_ref[:, cols]
        y = (u_ref[SUBLANES - 2:SUBLANES - 2 + tm, :] * w[0:1]
             + u_ref[SUBLANES - 1:SUBLANES - 1 + tm, :] * w[1:2]
             + u * w[2:3])
        return y + cb_ref[:, cols]

    def body(act_w, x_w, act_r, x_r):
        mixed = jnp.concatenate([_rms(oa_ref[...]) * ag_ref[...],
                                 _rms(ob_ref[...]) * bg_ref[...]], axis=-1)
        x1 = x_ref[...] + jnp.dot(mixed.astype(BF16), wout_ref[...],
                                  preferred_element_type=F32)
        x_w[...] = x1
        h2 = (_rms(x1) * fg_ref[...]).astype(BF16)

        for j in range(n_chunks):
            gcols = slice(j * tf, (j + 1) * tf)
            vcols = slice(D_FF + j * tf, D_FF + (j + 1) * tf)
            ug = jnp.dot(h2, wup_ref[:, gcols], preferred_element_type=F32)
            uv = jnp.dot(h2, wup_ref[:, vcols], preferred_element_type=F32)
            gate = conv(u_ref.at[(2 * j) % FFN_U_BUFS], ug, gcols)
            val = conv(u_ref.at[(2 * j + 1) % FFN_U_BUFS], uv, vcols)
            act_w[:, gcols] = (gate * (1.0 / (1.0 + jnp.exp(-gate))) * val).astype(BF16)
            for q in slab_after.get(j, ()):
                cols = slice(q * FFN_SLAB, (q + 1) * FFN_SLAB)
                o_ref[:, cols] = x_r[:, cols] + jnp.dot(act_r[...], wdn_ref[:, cols],
                                                        preferred_element_type=F32)
        o_ref[...] = _rms(o_ref[...]) * ng_ref[...]

    pl.when(step % 2 == 0)(functools.partial(body, act0_ref, xa_ref, act1_ref, xb_ref))
    pl.when(step % 2 == 1)(functools.partial(body, act1_ref, xb_ref, act0_ref, xa_ref))


def _ffn_call(x2, oa, ob, ag, bg, wout, fg, wup, cw, cb, wdn, ng, tiles_per_seq):
    t = x2.shape[0]
    tm = FFN_TM
    n_tiles = t // tm
    row = lambda i: (jnp.minimum(i, n_tiles - 1), 0)
    out_row = lambda i: (jnp.maximum(i - 1, 0), 0)
    fixed = lambda i: (0, 0)
    full = lambda a: pl.BlockSpec(a.shape, fixed, pipeline_mode=pl.Buffered(1))
    return pl.pallas_call(
        functools.partial(_ffn_kernel, tiles_per_seq),
        out_shape=jax.ShapeDtypeStruct((t, D_MODEL), F32),
        grid=(n_tiles + 1,),
        in_specs=[pl.BlockSpec((tm, D_MODEL), row), pl.BlockSpec((tm, A_WIDTH), row),
                  pl.BlockSpec((tm, B_WIDTH), row), full(ag), full(bg), full(wout), full(fg),
                  full(wup), full(cw), full(cb), full(wdn), full(ng)],
        out_specs=pl.BlockSpec((tm, D_MODEL), out_row),
        scratch_shapes=[pltpu.VMEM((SUBLANES, 2 * D_FF), F32),
                        pltpu.VMEM((FFN_U_BUFS, tm + SUBLANES, FFN_TF), F32),
                        pltpu.VMEM((tm, D_FF), BF16),
                        pltpu.VMEM((tm, D_FF), BF16),
                        pltpu.VMEM((tm, D_MODEL), F32),
                        pltpu.VMEM((tm, D_MODEL), F32)],
        compiler_params=pltpu.CompilerParams(
            dimension_semantics=("arbitrary",), vmem_limit_bytes=VMEM_LIMIT),
        name="ffn",
    )(x2, oa, ob, ag, bg, wout, fg, wup, cw, cb, wdn, ng)


def _t5_bucket_matrix():
    q_idx = BLK + np.arange(BLK)
    k_idx = np.arange(2 * BLK)
    dist = q_idx[:, None] - k_idx[None, :]
    max_exact = NUM_BUCKETS // 2
    n = np.maximum(dist, 0)
    large = max_exact + (np.log(np.maximum(n, 1).astype(np.float32) / max_exact)
                         / math.log(T5_MAX_DIST / max_exact)
                         * (NUM_BUCKETS - max_exact)).astype(np.int32)
    large = np.minimum(large, NUM_BUCKETS - 1)
    bucket = np.where(n < max_exact, n, large)
    in_window = (dist >= 0) & (dist < WINDOW)
    return np.where(in_window, bucket, -1).astype(np.int32)


def _row(v):
    return v.reshape(1, -1).astype(F32)


def kernel(x, positions, rel_bias_table, attn_norm_g, w_in, sinks, q_norm_g, w_q_b, kv_norm_g,
           w_kv_b, a_out_norm_g, b_out_norm_g, w_out, ffn_norm_g, w_up, conv_w, conv_b, w_down,
           final_norm_g):
    b, s, d = x.shape
    t = b * s
    assert d == D_MODEL and s % max(PROJ_TM, SWA_ROWS, MLA_TQ, FFN_TM) == 0
    assert attn_norm_g.shape[0] == 1, "one trunk layer"
    l = 0

    zeros64 = jnp.zeros((D_MODEL, LANES - ROPE_DIM), F32)
    kp = w_in[l][:, _C_KPA:_C_KPA + ROPE_DIM]
    k1, k2 = kp[:, :ROPE_HALF], kp[:, ROPE_HALF:]
    w_in_x = jnp.concatenate([w_in[l][:, :_C_KPA], k1, k2, zeros64], axis=1).astype(BF16)
    wq = w_q_b[l].reshape(Q_LORA, B_HEADS, NOPE_DIM + ROPE_DIM)
    q1 = wq[:, :, NOPE_DIM:NOPE_DIM + ROPE_HALF]
    q2 = wq[:, :, NOPE_DIM + ROPE_HALF:]
    zq = jnp.zeros((Q_LORA, B_HEADS, LANES - ROPE_DIM), F32)
    wq_x = jnp.concatenate([wq[:, :, :NOPE_DIM], q1, q2, zq], axis=2)
    wq_x = wq_x.reshape(Q_LORA, B_HEADS * _QB_HEAD).astype(BF16)
    wkv = w_kv_b[l].reshape(KV_LORA, B_HEADS, NOPE_DIM + V_DIM)
    wkv_x = jnp.concatenate([wkv[:, :, :NOPE_DIM].reshape(KV_LORA, -1),
                             wkv[:, :, NOPE_DIM:].reshape(KV_LORA, -1)], axis=1).astype(BF16)

    inv_freq = ROPE_THETA ** (-jnp.arange(0, ROPE_DIM, 2, dtype=F32) / ROPE_DIM)
    freq = jnp.tile(inv_freq, LANES // ROPE_HALF).reshape(1, LANES)

    x2 = x.reshape(t, d)
    pos2 = positions.reshape(t, 1).astype(jnp.int32)

    qa, ka, va, qm, km, vb = _proj_call(
        x2, pos2, _row(attn_norm_g[l]), w_in_x, _row(q_norm_g[l]), wq_x,
        _row(kv_norm_g[l]), wkv_x, freq)

    bucket = jnp.asarray(_t5_bucket_matrix())
    out_a = _swa_call(sinks[l].astype(F32), rel_bias_table.astype(F32), bucket,
                      qa.reshape(b, s, -1), ka.reshape(b, s, -1), va.reshape(b, s, -1))
    out_b = _mla_call(qm.reshape(b, s, -1), km.reshape(b, s, -1), vb.reshape(b, s, -1))

    cw = jnp.concatenate([conv_w[l], jnp.zeros((SUBLANES - CONV_W, 2 * D_FF), F32)], axis=0)
    out = _ffn_call(
        x2, out_a.reshape(t, -1), out_b.reshape(t, -1), _row(a_out_norm_g[l]),
        _row(b_out_norm_g[l]), w_out[l].astype(BF16), _row(ffn_norm_g[l]),
        w_up[l].astype(BF16), cw.astype(F32), _row(conv_b[l]), w_down[l].astype(BF16),
        _row(final_norm_g), s // FFN_TM)
    return out.reshape(b, s, d)
```

```python
import functools
import math

import numpy as np
import jax
import jax.numpy as jnp
from jax import lax
from jax.experimental import pallas as pl
from jax.experimental.pallas import tpu as pltpu

F32 = jnp.float32
BF16 = jnp.bfloat16

D_MODEL = 1024
A_HEADS = 8
A_KV_HEADS = 2
A_HEAD_DIM = 64
A_GROUP = A_HEADS // A_KV_HEADS
WINDOW = 128
BLK = WINDOW
A_WIDTH = A_HEADS * A_HEAD_DIM
A_KV_WIDTH = A_KV_HEADS * A_HEAD_DIM
NUM_BUCKETS = 32
T5_MAX_DIST = 128
B_HEADS = 4
Q_LORA = 256
KV_LORA = 128
NOPE_DIM = 128
ROPE_DIM = 64
ROPE_HALF = ROPE_DIM // 2
V_DIM = 128
ROPE_THETA = 10000.0
B_WIDTH = B_HEADS * V_DIM
D_FF = 2816
CONV_W = 3
EPS = 1e-6
NEG = -1e30
LOG2E = math.log2(math.e)

LANES = 128
SUBLANES = 8
QK_PAD = 256
VMEM_LIMIT = 56 * 1024 * 1024

PROJ_TM = 1024
PROJ_TS = 512
SWA_ROWS = 1024
SWA_LAG = 2
MLA_TQ = 1024
MLA_TK = 1024
MLA_TS = 256
MLA_UNROLL = 8
FFN_TM = 512
FFN_TF = 256
FFN_SLAB = 256
FFN_U_BUFS = 4

_C_QA = 0
_C_KA = _C_QA + A_WIDTH
_C_VA = _C_KA + A_KV_WIDTH
_C_CQ = _C_VA + A_KV_WIDTH
_C_CKV = _C_CQ + Q_LORA
_C_KPA = _C_CKV + KV_LORA
_C_END = _C_KPA + LANES
_QB_HEAD = 2 * LANES


def _rms(x):
    return x * lax.rsqrt(jnp.mean(x * x, axis=-1, keepdims=True) + EPS)


def _proj_kernel(x_ref, pos_ref, g_ref, w_in_ref, qg_ref, wq_ref, kvg_ref, wkv_ref,
                 freq_ref, qa_ref, ka_ref, va_ref, qm_ref, km_ref, vb_ref,
                 cos_ref, sin_ref):
    quarter = PROJ_TM // 4
    lane = lax.broadcasted_iota(jnp.int32, (quarter, LANES), 1)
    pos = pos_ref[...].astype(F32)
    pad = jnp.zeros((LANES - pos.shape[0], LANES), F32)
    pos_t = jnp.concatenate([pos, pad], axis=0).T
    lane_sq = lax.broadcasted_iota(jnp.int32, (LANES, LANES), 1)
    parts = quarter // LANES
    pieces = []
    for part in range(parts):
        col = [jnp.broadcast_to(pos_t[:, j * parts + part:j * parts + part + 1], (LANES, LANES))
               for j in range(4)]
        pieces.append(jnp.where(lane_sq < ROPE_HALF, col[0],
                                jnp.where(lane_sq < 2 * ROPE_HALF, col[1],
                                          jnp.where(lane_sq < 3 * ROPE_HALF, col[2], col[3]))))
    pos_d = jnp.concatenate(pieces, axis=0)
    ang = pos_d * freq_ref[...]
    for table, dst_ref, sign in ((jnp.cos(ang), cos_ref, 1.0), (jnp.sin(ang), sin_ref, -1.0)):
        for j in range(4):
            at0 = pltpu.roll(table, (LANES - j * ROPE_HALF) % LANES, 1)
            at1 = pltpu.roll(table, (LANES + ROPE_HALF - j * ROPE_HALF) % LANES, 1)
            spread = jnp.where(lane < ROPE_HALF, sign * at0, at1)
            if dst_ref is sin_ref:
                spread = jnp.where(lane < ROPE_DIM, spread, 0.0)
            dst_ref[j * quarter:(j + 1) * quarter, :] = spread

    lane_s = lax.broadcasted_iota(jnp.int32, (PROJ_TS, LANES), 1)

    def swap(x):
        return jnp.where(lane_s < ROPE_HALF, pltpu.roll(x, LANES - ROPE_HALF, 1),
                         pltpu.roll(x, ROPE_HALF, 1))

    n_st = PROJ_TM // PROJ_TS
    rows = [slice(st * PROJ_TS, (st + 1) * PROJ_TS) for st in range(n_st)]
    half = LANES // 2
    projs = []
    for r in rows:
        h = _rms(x_ref[r, :]) * g_ref[...]
        projs.append(jnp.dot(h.astype(BF16), w_in_ref[...], preferred_element_type=F32))

    ups = []
    for r, proj in zip(rows, projs):
        qa_ref[r, :] = (proj[:, _C_QA:_C_KA] * (A_HEAD_DIM ** -0.5 * LOG2E)).astype(BF16)
        for dst_ref, c0 in ((ka_ref, _C_KA), (va_ref, _C_VA)):
            nat = proj[:, c0:c0 + A_KV_WIDTH]
            dst_ref[r, :A_KV_WIDTH] = nat.astype(BF16)
            dst_ref[r, A_KV_WIDTH:] = pltpu.roll(nat, half, 1).astype(BF16)
        qn = _rms(proj[:, _C_CQ:_C_CKV]) * qg_ref[...]
        qb = jnp.dot(qn.astype(BF16), wq_ref[...], preferred_element_type=F32)
        kvn = _rms(proj[:, _C_CKV:_C_KPA]) * kvg_ref[...]
        kv = jnp.dot(kvn.astype(BF16), wkv_ref[...], preferred_element_type=F32)
        ups.append((qb, kv))

    scale = (NOPE_DIM + ROPE_DIM) ** -0.5 * LOG2E
    for r, proj, (qb, kv) in zip(rows, projs, ups):
        cc = cos_ref[r, :]
        ss = sin_ref[r, :]
        k_rot = proj[:, _C_KPA:_C_END]
        kpe = (k_rot * cc + swap(k_rot) * ss).astype(BF16)
        for hd in range(B_HEADS):
            c0 = hd * _QB_HEAD
            nope = qb[:, c0:c0 + LANES]
            q_rot = qb[:, c0 + LANES:c0 + 2 * LANES]
            o0 = hd * QK_PAD
            qm_ref[r, o0:o0 + LANES] = (nope * scale).astype(BF16)
            qm_ref[r, o0 + LANES:o0 + 2 * LANES] = (
                (q_rot * cc + swap(q_rot) * ss) * scale).astype(BF16)
            km_ref[r, o0:o0 + LANES] = kv[:, hd * NOPE_DIM:(hd + 1) * NOPE_DIM].astype(BF16)
            km_ref[r, o0 + LANES:o0 + 2 * LANES] = kpe
        vb_ref[r, :] = kv[:, B_HEADS * NOPE_DIM:].astype(BF16)


def _proj_call(x2, pos2, g, w_in, qg, wq, kvg, wkv, freq):
    t = x2.shape[0]
    tm = PROJ_TM
    row = lambda i: (i, 0)
    fixed = lambda i: (0, 0)
    full = lambda a: pl.BlockSpec(a.shape, fixed)
    out_shape = (
        jax.ShapeDtypeStruct((t, A_WIDTH), BF16),
        jax.ShapeDtypeStruct((t, 2 * A_KV_WIDTH), BF16),
        jax.ShapeDtypeStruct((t, 2 * A_KV_WIDTH), BF16),
        jax.ShapeDtypeStruct((t, B_HEADS * QK_PAD), BF16),
        jax.ShapeDtypeStruct((t, B_HEADS * QK_PAD), BF16),
        jax.ShapeDtypeStruct((t, B_WIDTH), BF16),
    )
    return pl.pallas_call(
        _proj_kernel,
        out_shape=out_shape,
        grid=(t // tm,),
        in_specs=[pl.BlockSpec((tm, D_MODEL), row), pl.BlockSpec((tm // LANES, LANES), row),
                  full(g), full(w_in), full(qg), full(wq), full(kvg), full(wkv),
                  full(freq)],
        out_specs=tuple(pl.BlockSpec((tm, s.shape[1]), row) for s in out_shape),
        scratch_shapes=[pltpu.VMEM((tm, LANES), F32), pltpu.VMEM((tm, LANES), F32)],
        compiler_params=pltpu.CompilerParams(
            dimension_semantics=("arbitrary",), vmem_limit_bytes=VMEM_LIMIT),
        name="proj",
    )(x2, pos2, g, w_in, qg, wq, kvg, wkv, freq)


def _swa_kernel(sinks_ref, tbl_ref, bucket_ref, q_ref, kp_ref, kc_ref, vp_ref, vc_ref,
                o_ref, bias_ref, p_ref, t_ref):
    step = pl.program_id(1)
    half = LANES // 2

    @pl.when((pl.program_id(0) == 0) & (step == 0))
    def _():
        bk = bucket_ref[...]
        for hd in range(A_HEADS):
            acc = jnp.full((BLK, 2 * BLK), NEG, F32)
            for b in range(NUM_BUCKETS):
                acc = jnp.where(bk == b, tbl_ref[b, hd] * LOG2E, acc)
            bias_ref[hd] = acc

    kidx = lax.broadcasted_iota(jnp.int32, (BLK, 2 * BLK), 1)
    pad_mask = kidx < jnp.where(step == 0, BLK, 0)

    def lane_mask(n_rows, low):
        lane = lax.broadcasted_iota(jnp.int32, (n_rows, LANES), 1)
        return jnp.where((lane < half) == low, 1.0, 0.0).astype(BF16)

    n_keys = BLK + SWA_ROWS
    lo, hi = lane_mask(n_keys, True), lane_mask(n_keys, False)
    ones_low, ones_high = lane_mask(2 * BLK, True), lane_mask(2 * BLK, False)
    lo_q = lax.broadcasted_iota(jnp.int32, (BLK, LANES), 1) < half

    k_all = jnp.concatenate([kp_ref[...], kc_ref[...]], axis=0)
    v_all = jnp.concatenate([vp_ref[...], vc_ref[...]], axis=0)
    k_nat, k_swp = k_all[:, :LANES], k_all[:, LANES:]
    v_nat, v_swp = v_all[:, :LANES], v_all[:, LANES:]
    k_low, k_high = (k_nat * lo, k_swp * lo), (k_swp * hi, k_nat * hi)
    v_low, v_high = (v_nat * lo, v_swp * lo), (v_swp * hi, v_nat * hi)

    dims = (((1,), (1,)), ((), ()))
    n_blk = SWA_ROWS // BLK
    pairs = A_GROUP // 2

    def pass1(r):
        keys = slice(r * BLK, (r + 2) * BLK)
        q_blk = q_ref[r * BLK:(r + 1) * BLK, :]
        for g in range(A_KV_HEADS):
            qg = jnp.concatenate([q_blk[:, (2 * g) * LANES:(2 * g + 1) * LANES],
                                  q_blk[:, (2 * g + 1) * LANES:(2 * g + 2) * LANES]], axis=0)
            s_pair = [lax.dot_general(qg, k_side[g][keys], dims, preferred_element_type=F32)
                      for k_side in (k_low, k_high)]
            for j in range(pairs):
                pair = (r * A_KV_HEADS + g) * pairs + j
                sink_terms = []
                for e in range(2):
                    hd = A_GROUP * g + 2 * j + e
                    s = s_pair[e][j * BLK:(j + 1) * BLK, :]
                    s = s + bias_ref[hd]
                    if r == 0:
                        s = jnp.where(pad_mask, NEG, s)
                    sink = sinks_ref[hd] * LOG2E
                    m = jnp.maximum(jnp.max(s, axis=-1, keepdims=True), sink)
                    p_ref[pair, :, e * 2 * BLK:(e + 1) * 2 * BLK] = jnp.exp2(s - m).astype(BF16)
                    sink_terms.append(jnp.exp2(sink - m))
                t_ref[pair] = jnp.where(lo_q, sink_terms[0], sink_terms[1])

    def pass2(r):
        rows = slice(r * BLK, (r + 1) * BLK)
        keys = slice(r * BLK, (r + 2) * BLK)
        for g in range(A_KV_HEADS):
            rhs = jnp.concatenate(
                [jnp.concatenate([v_low[g][keys], ones_low], axis=1),
                 jnp.concatenate([v_high[g][keys], ones_high], axis=1)], axis=0)
            for j in range(pairs):
                pair = (r * A_KV_HEADS + g) * pairs + j
                res = jnp.dot(p_ref[pair], rhs, preferred_element_type=F32)
                col = (g * pairs + j) * LANES
                o_ref[rows, col:col + LANES] = res[:, :LANES] / (res[:, LANES:] + t_ref[pair])

    for r in range(n_blk + SWA_LAG):
        if r < n_blk:
            pass1(r)
        if r >= SWA_LAG:
            pass2(r - SWA_LAG)


def _swa_call(sinks, table, bucket, qa, kx, vx):
    b, s, _ = qa.shape
    rows = SWA_ROWS
    per = rows // BLK
    cur = lambda bi, i: (bi, i, 0)
    prev = lambda bi, i: (bi, jnp.maximum(i * per - 1, 0), 0)
    smem = pl.BlockSpec(memory_space=pltpu.SMEM)
    return pl.pallas_call(
        _swa_kernel,
        out_shape=jax.ShapeDtypeStruct((b, s, A_WIDTH), F32),
        grid=(b, s // rows),
        in_specs=[smem, smem, pl.BlockSpec(bucket.shape, lambda bi, i: (0, 0)),
                  pl.BlockSpec((None, rows, A_WIDTH), cur),
                  pl.BlockSpec((None, BLK, 2 * A_KV_WIDTH), prev),
                  pl.BlockSpec((None, rows, 2 * A_KV_WIDTH), cur),
                  pl.BlockSpec((None, BLK, 2 * A_KV_WIDTH), prev),
                  pl.BlockSpec((None, rows, 2 * A_KV_WIDTH), cur)],
        out_specs=pl.BlockSpec((None, rows, A_WIDTH), cur),
        scratch_shapes=[pltpu.VMEM((A_HEADS, BLK, 2 * BLK), F32),
                        pltpu.VMEM((per * A_HEADS // 2, BLK, 4 * BLK), BF16),
                        pltpu.VMEM((per * A_HEADS // 2, BLK, LANES), F32)],
        compiler_params=pltpu.CompilerParams(
            dimension_semantics=("arbitrary", "arbitrary"), vmem_limit_bytes=VMEM_LIMIT),
        name="swa",
    )(sinks, table, bucket, qa, kx, kx, vx, vx)


def _mla_kernel(q_ref, qn_ref, k_ref, v_ref, o_ref,
                s0_ref, s1_ref, s2_ref, x0_ref, x1_ref, x2_ref, m_ref, acc_ref):
    qi = pl.program_id(2)
    n_q = pl.num_programs(2)
    tq, tk, ts = MLA_TQ, MLA_TK, MLA_TS
    streams = tq // ts
    buf0 = (s0_ref, x0_ref)
    buf1 = (s1_ref, x1_ref)
    buf2 = (s2_ref, x2_ref)

    def scores(c, dst, queries=q_ref):
        s_ref, x_ref = dst
        start = pl.multiple_of(c * tk, tk)
        k = k_ref[pl.ds(start, tk), :]
        for h in range(streams):
            s = lax.dot_general(queries[h * ts:(h + 1) * ts, :], k,
                                (((1,), (1,)), ((), ())), preferred_element_type=F32)
            s_ref[h] = s
            blk = s[:, 0:LANES]
            for j in range(1, tk // LANES):
                blk = jnp.maximum(blk, s[:, j * LANES:(j + 1) * LANES])
            x_ref[h] = blk

    def accumulate(c, src, diagonal):
        s_ref, x_ref = src
        start = pl.multiple_of(c * tk, tk)
        for h in range(streams):
            nk = (h + 1) * ts if diagonal else tk
            v = v_ref[pl.ds(start, nk), :]
            v1 = jnp.concatenate([v, jnp.ones_like(v)], axis=1)
            s = s_ref[h, :, 0:nk]
            if diagonal:
                qrow = h * ts + lax.broadcasted_iota(jnp.int32, (ts, nk), 0)
                kcol = lax.broadcasted_iota(jnp.int32, (ts, nk), 1)
                s = jnp.where(qrow >= kcol, s, NEG)
                row_max = jnp.max(s, axis=-1, keepdims=True)
            else:
                row_max = jnp.max(x_ref[h], axis=-1, keepdims=True)
            m = m_ref[h]
            m_new = jnp.maximum(m, row_max)
            alpha = jnp.exp2(m - m_new)
            p = jnp.exp2(s - m_new).astype(BF16)
            acc_ref[h] = alpha * acc_ref[h] + jnp.dot(p, v1, preferred_element_type=F32)
            m_ref[h] = m_new

    def tick(c, src, dst):
        scores(c + 1, dst)
        accumulate(c, src, False)

    m_ref[...] = jnp.full(m_ref.shape, NEG, F32)
    acc_ref[...] = jnp.zeros(acc_ref.shape, F32)

    odd = (qi % 2) == 1
    even_later = (qi > 0) & jnp.logical_not(odd)

    @pl.when(qi == 0)
    def _():
        scores(0, buf0)

    @pl.when(odd)
    def _():
        tick(0, buf2, buf0)

    @pl.when(even_later)
    def _():
        tick(0, buf2, buf1)
        tick(1, buf1, buf0)

    def ticks(c, count):
        for i in range(0, count, 2):
            tick(c + i, buf0, buf1)
            tick(c + i + 1, buf1, buf0)

    done = jnp.where(qi == 0, 0, jnp.where(odd, 1, 2))
    rest = qi - done
    size = 2
    while size < MLA_UNROLL:
        pl.when((rest & size) != 0)(functools.partial(ticks, done, size))
        done = done + (rest & size)
        size *= 2

    def group(t, carry):
        ticks(done + MLA_UNROLL * t, MLA_UNROLL)
        return carry

    lax.fori_loop(0, rest // MLA_UNROLL, group, 0)

    @pl.when(qi + 1 < n_q)
    def _():
        scores(0, buf2, qn_ref)
        accumulate(qi, buf0, True)

    @pl.when(qi + 1 == n_q)
    def _():
        accumulate(qi, buf0, True)

    for h in range(streams):
        acc = acc_ref[h]
        o_ref[h * ts:(h + 1) * ts, :] = acc[:, :V_DIM] / acc[:, V_DIM:]


def _mla_call(qm, km, vb):
    b, s, _ = qm.shape
    tq, tk, ts = MLA_TQ, MLA_TK, MLA_TS
    assert tq == tk and tq % ts == 0
    streams = tq // ts
    n_q = s // tq
    score_buf = pltpu.VMEM((streams, ts, tk), F32)
    max_buf = pltpu.VMEM((streams, ts, LANES), F32)
    return pl.pallas_call(
        _mla_kernel,
        out_shape=jax.ShapeDtypeStruct((b, s, B_WIDTH), F32),
        grid=(b, B_HEADS, n_q),
        in_specs=[pl.BlockSpec((None, tq, QK_PAD), lambda bi, h, i: (bi, i, h)),
                  pl.BlockSpec((None, tq, QK_PAD),
                               lambda bi, h, i: (bi, jnp.minimum(i + 1, n_q - 1), h)),
                  pl.BlockSpec((None, s, QK_PAD), lambda bi, h, i: (bi, 0, h)),
                  pl.BlockSpec((None, s, V_DIM), lambda bi, h, i: (bi, 0, h))],
        out_specs=pl.BlockSpec((None, tq, V_DIM), lambda bi, h, i: (bi, i, h)),
        scratch_shapes=[score_buf, score_buf, score_buf, max_buf, max_buf, max_buf,
                        pltpu.VMEM((streams, ts, 1), F32),
                        pltpu.VMEM((streams, ts, 2 * V_DIM), F32)],
        compiler_params=pltpu.CompilerParams(
            dimension_semantics=("arbitrary", "arbitrary", "arbitrary"),
            vmem_limit_bytes=VMEM_LIMIT),
        name="mla",
    )(qm, qm, km, vb)


def _ffn_kernel(tiles_per_seq, x_ref, oa_ref, ob_ref, ag_ref, bg_ref, wout_ref, fg_ref,
                wup_ref, cw_ref, cb_ref, wdn_ref, ng_ref, o_ref,
                carry_ref, u_ref, act0_ref, act1_ref, xa_ref, xb_ref):
    tm, tf = FFN_TM, FFN_TF
    step = pl.program_id(0)
    n_chunks = D_FF // tf
    n_slabs = D_MODEL // FFN_SLAB
    slab_after = {int((q + 0.5) * n_chunks / n_slabs): (q,) for q in range(n_slabs)}
    assert len(slab_after) == n_slabs

    @pl.when(step == 0)
    def _():
        act1_ref[...] = jnp.zeros_like(act1_ref)
        xb_ref[...] = jnp.zeros_like(xb_ref)

    @pl.when((step % tiles_per_seq) == 0)
    def _():
        carry_ref[...] = jnp.zeros_like(carry_ref)

    def conv(u_ref, u, cols):
        u_ref[0:SUBLANES, :] = carry_ref[:, cols]
        u_ref[SUBLANES:SUBLANES + tm, :] = u
        carry_ref[:, cols] = u_ref[tm:tm + SUBLANES, :]
        w = cw_ref[:, cols]
        y = (u_ref[SUBLANES - 2:SUBLANES - 2 + tm, :] * w[0:1]
             + u_ref[SUBLANES - 1:SUBLANES - 1 + tm, :] * w[1:2]
             + u * w[2:3])
        return y + cb_ref[:, cols]

    def body(act_w, x_w, act_r, x_r):
        mixed = jnp.concatenate([_rms(oa_ref[...]) * ag_ref[...],
                                 _rms(ob_ref[...]) * bg_ref[...]], axis=-1)
        x1 = x_ref[...] + jnp.dot(mixed.astype(BF16), wout_ref[...],
                                  preferred_element_type=F32)
        x_w[...] = x1
        h2 = (_rms(x1) * fg_ref[...]).astype(BF16)

        for j in range(n_chunks):
            gcols = slice(j * tf, (j + 1) * tf)
            vcols = slice(D_FF + j * tf, D_FF + (j + 1) * tf)
            ug = jnp.dot(h2, wup_ref[:, gcols], preferred_element_type=F32)
            uv = jnp.dot(h2, wup_ref[:, vcols], preferred_element_type=F32)
            gate = conv(u_ref.at[(2 * j) % FFN_U_BUFS], ug, gcols)
            val = conv(u_ref.at[(2 * j + 1) % FFN_U_BUFS], uv, vcols)
            act_w[:, gcols] = (gate * (1.0 / (1.0 + jnp.exp(-gate))) * val).astype(BF16)
            for q in slab_after.get(j, ()):
                cols = slice(q * FFN_SLAB, (q + 1) * FFN_SLAB)
                o_ref[:, cols] = x_r[:, cols] + jnp.dot(act_r[...], wdn_ref[:, cols],
                                                        preferred_element_type=F32)
        o_ref[...] = _rms(o_ref[...]) * ng_ref[...]

    pl.when(step % 2 == 0)(functools.partial(body, act0_ref, xa_ref, act1_ref, xb_ref))
    pl.when(step % 2 == 1)(functools.partial(body, act1_ref, xb_ref, act0_ref, xa_ref))


def _ffn_call(x2, oa, ob, ag, bg, wout, fg, wup, cw, cb, wdn, ng, tiles_per_seq):
    t = x2.shape[0]
    tm = FFN_TM
    n_tiles = t // tm
    row = lambda i: (jnp.minimum(i, n_tiles - 1), 0)
    out_row = lambda i: (jnp.maximum(i - 1, 0), 0)
    fixed = lambda i: (0, 0)
    full = lambda a: pl.BlockSpec(a.shape, fixed, pipeline_mode=pl.Buffered(1))
    return pl.pallas_call(
        functools.partial(_ffn_kernel, tiles_per_seq),
        out_shape=jax.ShapeDtypeStruct((t, D_MODEL), F32),
        grid=(n_tiles + 1,),
        in_specs=[pl.BlockSpec((tm, D_MODEL), row), pl.BlockSpec((tm, A_WIDTH), row),
                  pl.BlockSpec((tm, B_WIDTH), row), full(ag), full(bg), full(wout), full(fg),
                  full(wup), full(cw), full(cb), full(wdn), full(ng)],
        out_specs=pl.BlockSpec((tm, D_MODEL), out_row),
        scratch_shapes=[pltpu.VMEM((SUBLANES, 2 * D_FF), F32),
                        pltpu.VMEM((FFN_U_BUFS, tm + SUBLANES, FFN_TF), F32),
                        pltpu.VMEM((tm, D_FF), BF16),
                        pltpu.VMEM((tm, D_FF), BF16),
                        pltpu.VMEM((tm, D_MODEL), F32),
                        pltpu.VMEM((tm, D_MODEL), F32)],
        compiler_params=pltpu.CompilerParams(
            dimension_semantics=("arbitrary",), vmem_limit_bytes=VMEM_LIMIT),
        name="ffn",
    )(x2, oa, ob, ag, bg, wout, fg, wup, cw, cb, wdn, ng)


def _t5_bucket_matrix():
    q_idx = BLK + np.arange(BLK)
    k_idx = np.arange(2 * BLK)
    dist = q_idx[:, None] - k_idx[None, :]
    max_exact = NUM_BUCKETS // 2
    n = np.maximum(dist, 0)
    large = max_exact + (np.log(np.maximum(n, 1).astype(np.float32) / max_exact)
                         / math.log(T5_MAX_DIST / max_exact)
                         * (NUM_BUCKETS - max_exact)).astype(np.int32)
    large = np.minimum(large, NUM_BUCKETS - 1)
    bucket = np.where(n < max_exact, n, large)
    in_window = (dist >= 0) & (dist < WINDOW)
    return np.where(in_window, bucket, -1).astype(np.int32)


def _row(v):
    return v.reshape(1, -1).astype(F32)


def kernel(x, positions, rel_bias_table, attn_norm_g, w_in, sinks, q_norm_g, w_q_b, kv_norm_g,
           w_kv_b, a_out_norm_g, b_out_norm_g, w_out, ffn_norm_g, w_up, conv_w, conv_b, w_down,
           final_norm_g):
    b, s, d = x.shape
    t = b * s
    assert d == D_MODEL and s % max(PROJ_TM, SWA_ROWS, MLA_TQ, FFN_TM) == 0
    assert attn_norm_g.shape[0] == 1, "one trunk layer"
    l = 0

    zeros64 = jnp.zeros((D_MODEL, LANES - ROPE_DIM), F32)
    kp = w_in[l][:, _C_KPA:_C_KPA + ROPE_DIM]
    k1, k2 = kp[:, :ROPE_HALF], kp[:, ROPE_HALF:]
    w_in_x = jnp.concatenate([w_in[l][:, :_C_KPA], k1, k2, zeros64], axis=1).astype(BF16)
    wq = w_q_b[l].reshape(Q_LORA, B_HEADS, NOPE_DIM + ROPE_DIM)
    q1 = wq[:, :, NOPE_DIM:NOPE_DIM + ROPE_HALF]
    q2 = wq[:, :, NOPE_DIM + ROPE_HALF:]
    zq = jnp.zeros((Q_LORA, B_HEADS, LANES - ROPE_DIM), F32)
    wq_x = jnp.concatenate([wq[:, :, :NOPE_DIM], q1, q2, zq], axis=2)
    wq_x = wq_x.reshape(Q_LORA, B_HEADS * _QB_HEAD).astype(BF16)
    wkv = w_kv_b[l].reshape(KV_LORA, B_HEADS, NOPE_DIM + V_DIM)
    wkv_x = jnp.concatenate([wkv[:, :, :NOPE_DIM].reshape(KV_LORA, -1),
                             wkv[:, :, NOPE_DIM:].reshape(KV_LORA, -1)], axis=1).astype(BF16)

    inv_freq = ROPE_THETA ** (-jnp.arange(0, ROPE_DIM, 2, dtype=F32) / ROPE_DIM)
    freq = jnp.tile(inv_freq, LANES // ROPE_HALF).reshape(1, LANES)

    x2 = x.reshape(t, d)
    pos2 = positions.reshape(t // LANES, LANES).astype(jnp.int32)

    qa, ka, va, qm, km, vb = _proj_call(
        x2, pos2, _row(attn_norm_g[l]), w_in_x, _row(q_norm_g[l]), wq_x,
        _row(kv_norm_g[l]), wkv_x, freq)

    bucket = jnp.asarray(_t5_bucket_matrix())
    out_a = _swa_call(sinks[l].astype(F32), rel_bias_table.astype(F32), bucket,
                      qa.reshape(b, s, -1), ka.reshape(b, s, -1), va.reshape(b, s, -1))
    out_b = _mla_call(qm.reshape(b, s, -1), km.reshape(b, s, -1), vb.reshape(b, s, -1))

    cw = jnp.concatenate([conv_w[l], jnp.zeros((SUBLANES - CONV_W, 2 * D_FF), F32)], axis=0)
    out = _ffn_call(
        x2, out_a.reshape(t, -1), out_b.reshape(t, -1), _row(a_out_norm_g[l]),
        _row(b_out_norm_g[l]), w_out[l].astype(BF16), _row(ffn_norm_g[l]),
        w_up[l].astype(BF16), cw.astype(F32), _row(conv_b[l]), w_down[l].astype(BF16),
        _row(final_norm_g), s // FFN_TM)
    return out.reshape(b, s, d)
```

```python
import functools
import math

import numpy as np
import jax
import jax.numpy as jnp
from jax import lax
from jax.experimental import pallas as pl
from jax.experimental.pallas import tpu as pltpu

F32 = jnp.float32
BF16 = jnp.bfloat16

D_MODEL = 1024
A_HEADS = 8
A_KV_HEADS = 2
A_HEAD_DIM = 64
A_GROUP = A_HEADS // A_KV_HEADS
WINDOW = 128
BLK = WINDOW
A_WIDTH = A_HEADS * A_HEAD_DIM
A_KV_WIDTH = A_KV_HEADS * A_HEAD_DIM
NUM_BUCKETS = 32
T5_MAX_DIST = 128
B_HEADS = 4
Q_LORA = 256
KV_LORA = 128
NOPE_DIM = 128
ROPE_DIM = 64
ROPE_HALF = ROPE_DIM // 2
V_DIM = 128
ROPE_THETA = 10000.0
B_WIDTH = B_HEADS * V_DIM
D_FF = 2816
CONV_W = 3
EPS = 1e-6
NEG = -1e30
LOG2E = math.log2(math.e)

LANES = 128
SUBLANES = 8
QK_PAD = 256
VMEM_LIMIT = 56 * 1024 * 1024

PROJ_TM = 1024
PROJ_TS = 512
SWA_ROWS = 1024
MLA_TQ = 1024
MLA_TK = 1024
MLA_TS = 256
MLA_UNROLL = 8
FFN_TM = 512
FFN_TF = 256
FFN_SLAB = 256

_C_QA = 0
_C_KA = _C_QA + A_WIDTH
_C_VA = _C_KA + A_KV_WIDTH
_C_CQ = _C_VA + A_KV_WIDTH
_C_CKV = _C_CQ + Q_LORA
_C_KPA = _C_CKV + KV_LORA
_C_END = _C_KPA + LANES
_QB_HEAD = 2 * LANES


def _rms(x):
    return x * lax.rsqrt(jnp.mean(x * x, axis=-1, keepdims=True) + EPS)


def _proj_kernel(x_ref, pos_ref, g_ref, w_in_ref, qg_ref, wq_ref, kvg_ref, wkv_ref,
                 freq_ref, qa_ref, ka_ref, va_ref, qm_ref, km_ref, vb_ref,
                 cos_ref, sin_ref):
    quarter = PROJ_TM // 4
    lane = lax.broadcasted_iota(jnp.int32, (quarter, LANES), 1)
    pos_q = [pos_ref[j * quarter:(j + 1) * quarter, :].astype(F32) for j in range(4)]
    pos_d = jnp.where(lane < ROPE_HALF, pos_q[0],
                      jnp.where(lane < 2 * ROPE_HALF, pos_q[1],
                                jnp.where(lane < 3 * ROPE_HALF, pos_q[2], pos_q[3])))
    ang = pos_d * freq_ref[...]
    for table, dst_ref, sign in ((jnp.cos(ang), cos_ref, 1.0), (jnp.sin(ang), sin_ref, -1.0)):
        for j in range(4):
            at0 = pltpu.roll(table, (LANES - j * ROPE_HALF) % LANES, 1)
            at1 = pltpu.roll(table, (LANES + ROPE_HALF - j * ROPE_HALF) % LANES, 1)
            spread = jnp.where(lane < ROPE_HALF, sign * at0, at1)
            if dst_ref is sin_ref:
                spread = jnp.where(lane < ROPE_DIM, spread, 0.0)
            dst_ref[j * quarter:(j + 1) * quarter, :] = spread

    lane_s = lax.broadcasted_iota(jnp.int32, (PROJ_TS, LANES), 1)

    def swap(x):
        return jnp.where(lane_s < ROPE_HALF, pltpu.roll(x, LANES - ROPE_HALF, 1),
                         pltpu.roll(x, ROPE_HALF, 1))

    n_st = PROJ_TM // PROJ_TS
    rows = [slice(st * PROJ_TS, (st + 1) * PROJ_TS) for st in range(n_st)]
    half = LANES // 2
    projs = []
    for r in rows:
        h = _rms(x_ref[r, :]) * g_ref[...]
        projs.append(jnp.dot(h.astype(BF16), w_in_ref[...], preferred_element_type=F32))

    ups = []
    for r, proj in zip(rows, projs):
        qa_ref[r, :] = (proj[:, _C_QA:_C_KA] * (A_HEAD_DIM ** -0.5 * LOG2E)).astype(BF16)
        for dst_ref, c0 in ((ka_ref, _C_KA), (va_ref, _C_VA)):
            nat = proj[:, c0:c0 + A_KV_WIDTH]
            dst_ref[r, :A_KV_WIDTH] = nat.astype(BF16)
            dst_ref[r, A_KV_WIDTH:] = pltpu.roll(nat, half, 1).astype(BF16)
        qn = _rms(proj[:, _C_CQ:_C_CKV]) * qg_ref[...]
        qb = jnp.dot(qn.astype(BF16), wq_ref[...], preferred_element_type=F32)
        kvn = _rms(proj[:, _C_CKV:_C_KPA]) * kvg_ref[...]
        kv = jnp.dot(kvn.astype(BF16), wkv_ref[...], preferred_element_type=F32)
        ups.append((qb, kv))

    scale = (NOPE_DIM + ROPE_DIM) ** -0.5 * LOG2E
    for r, proj, (qb, kv) in zip(rows, projs, ups):
        cc = cos_ref[r, :]
        ss = sin_ref[r, :]
        k_rot = proj[:, _C_KPA:_C_END]
        kpe = (k_rot * cc + swap(k_rot) * ss).astype(BF16)
        for hd in range(B_HEADS):
            c0 = hd * _QB_HEAD
            nope = qb[:, c0:c0 + LANES]
            q_rot = qb[:, c0 + LANES:c0 + 2 * LANES]
            o0 = hd * QK_PAD
            qm_ref[r, o0:o0 + LANES] = (nope * scale).astype(BF16)
            qm_ref[r, o0 + LANES:o0 + 2 * LANES] = (
                (q_rot * cc + swap(q_rot) * ss) * scale).astype(BF16)
            km_ref[r, o0:o0 + LANES] = kv[:, hd * NOPE_DIM:(hd + 1) * NOPE_DIM].astype(BF16)
            km_ref[r, o0 + LANES:o0 + 2 * LANES] = kpe
        vb_ref[r, :] = kv[:, B_HEADS * NOPE_DIM:].astype(BF16)


def _proj_call(x2, pos2, g, w_in, qg, wq, kvg, wkv, freq):
    t = x2.shape[0]
    tm = PROJ_TM
    row = lambda i: (i, 0)
    fixed = lambda i: (0, 0)
    full = lambda a: pl.BlockSpec(a.shape, fixed)
    out_shape = (
        jax.ShapeDtypeStruct((t, A_WIDTH), BF16),
        jax.ShapeDtypeStruct((t, 2 * A_KV_WIDTH), BF16),
        jax.ShapeDtypeStruct((t, 2 * A_KV_WIDTH), BF16),
        jax.ShapeDtypeStruct((t, B_HEADS * QK_PAD), BF16),
        jax.ShapeDtypeStruct((t, B_HEADS * QK_PAD), BF16),
        jax.ShapeDtypeStruct((t, B_WIDTH), BF16),
    )
    return pl.pallas_call(
        _proj_kernel,
        out_shape=out_shape,
        grid=(t // tm,),
        in_specs=[pl.BlockSpec((tm, D_MODEL), row), pl.BlockSpec((tm, 1), row),
                  full(g), full(w_in), full(qg), full(wq), full(kvg), full(wkv),
                  full(freq)],
        out_specs=tuple(pl.BlockSpec((tm, s.shape[1]), row) for s in out_shape),
        scratch_shapes=[pltpu.VMEM((tm, LANES), F32), pltpu.VMEM((tm, LANES), F32)],
        compiler_params=pltpu.CompilerParams(
            dimension_semantics=("arbitrary",), vmem_limit_bytes=VMEM_LIMIT),
        name="proj",
    )(x2, pos2, g, w_in, qg, wq, kvg, wkv, freq)


def _swa_kernel(sinks_ref, tbl_ref, bucket_ref, q_ref, kp_ref, kc_ref, vp_ref, vc_ref,
                o_ref, bias_ref, p_ref, t_ref):
    step = pl.program_id(1)
    half = LANES // 2

    @pl.when((pl.program_id(0) == 0) & (step == 0))
    def _():
        bk = bucket_ref[...]
        for hd in range(A_HEADS):
            acc = jnp.full((BLK, 2 * BLK), NEG, F32)
            for b in range(NUM_BUCKETS):
                acc = jnp.where(bk == b, tbl_ref[b, hd] * LOG2E, acc)
            bias_ref[hd] = acc

    kidx = lax.broadcasted_iota(jnp.int32, (BLK, 2 * BLK), 1)
    pad_mask = kidx < jnp.where(step == 0, BLK, 0)

    def lane_mask(n_rows, low):
        lane = lax.broadcasted_iota(jnp.int32, (n_rows, LANES), 1)
        return jnp.where((lane < half) == low, 1.0, 0.0).astype(BF16)

    n_keys = BLK + SWA_ROWS
    lo, hi = lane_mask(n_keys, True), lane_mask(n_keys, False)
    ones_low, ones_high = lane_mask(2 * BLK, True), lane_mask(2 * BLK, False)
    lo_q = lax.broadcasted_iota(jnp.int32, (BLK, LANES), 1) < half

    k_all = jnp.concatenate([kp_ref[...], kc_ref[...]], axis=0)
    v_all = jnp.concatenate([vp_ref[...], vc_ref[...]], axis=0)
    k_nat, k_swp = k_all[:, :LANES], k_all[:, LANES:]
    v_nat, v_swp = v_all[:, :LANES], v_all[:, LANES:]
    k_low, k_high = (k_nat * lo, k_swp * lo), (k_swp * hi, k_nat * hi)
    v_low, v_high = (v_nat * lo, v_swp * lo), (v_swp * hi, v_nat * hi)

    dims = (((1,), (1,)), ((), ()))
    n_blk = SWA_ROWS // BLK
    pairs = A_GROUP // 2

    def pass1(r):
        keys = slice(r * BLK, (r + 2) * BLK)
        q_blk = q_ref[r * BLK:(r + 1) * BLK, :]
        for g in range(A_KV_HEADS):
            qg = jnp.concatenate([q_blk[:, (2 * g) * LANES:(2 * g + 1) * LANES],
                                  q_blk[:, (2 * g + 1) * LANES:(2 * g + 2) * LANES]], axis=0)
            s_pair = [lax.dot_general(qg, k_side[g][keys], dims, preferred_element_type=F32)
                      for k_side in (k_low, k_high)]
            for j in range(pairs):
                pair = (r * A_KV_HEADS + g) * pairs + j
                sink_terms = []
                for e in range(2):
                    hd = A_GROUP * g + 2 * j + e
                    s = s_pair[e][j * BLK:(j + 1) * BLK, :]
                    s = s + bias_ref[hd]
                    if r == 0:
                        s = jnp.where(pad_mask, NEG, s)
                    sink = sinks_ref[hd] * LOG2E
                    m = jnp.maximum(jnp.max(s, axis=-1, keepdims=True), sink)
                    p_ref[pair, :, e * 2 * BLK:(e + 1) * 2 * BLK] = jnp.exp2(s - m).astype(BF16)
                    sink_terms.append(jnp.exp2(sink - m))
                t_ref[pair] = jnp.where(lo_q, sink_terms[0], sink_terms[1])

    def pass2(r):
        rows = slice(r * BLK, (r + 1) * BLK)
        keys = slice(r * BLK, (r + 2) * BLK)
        for g in range(A_KV_HEADS):
            rhs = jnp.concatenate(
                [jnp.concatenate([v_low[g][keys], ones_low], axis=1),
                 jnp.concatenate([v_high[g][keys], ones_high], axis=1)], axis=0)
            for j in range(pairs):
                pair = (r * A_KV_HEADS + g) * pairs + j
                res = jnp.dot(p_ref[pair], rhs, preferred_element_type=F32)
                col = (g * pairs + j) * LANES
                o_ref[rows, col:col + LANES] = res[:, :LANES] / (res[:, LANES:] + t_ref[pair])

    for r in range(n_blk):
        pass1(r)
    for r in range(n_blk):
        pass2(r)


def _swa_call(sinks, table, bucket, qa, kx, vx):
    b, s, _ = qa.shape
    rows = SWA_ROWS
    per = rows // BLK
    cur = lambda bi, i: (bi, i, 0)
    prev = lambda bi, i: (bi, jnp.maximum(i * per - 1, 0), 0)
    smem = pl.BlockSpec(memory_space=pltpu.SMEM)
    return pl.pallas_call(
        _swa_kernel,
        out_shape=jax.ShapeDtypeStruct((b, s, A_WIDTH), F32),
        grid=(b, s // rows),
        in_specs=[smem, smem, pl.BlockSpec(bucket.shape, lambda bi, i: (0, 0)),
                  pl.BlockSpec((None, rows, A_WIDTH), cur),
                  pl.BlockSpec((None, BLK, 2 * A_KV_WIDTH), prev),
                  pl.BlockSpec((None, rows, 2 * A_KV_WIDTH), cur),
                  pl.BlockSpec((None, BLK, 2 * A_KV_WIDTH), prev),
                  pl.BlockSpec((None, rows, 2 * A_KV_WIDTH), cur)],
        out_specs=pl.BlockSpec((None, rows, A_WIDTH), cur),
        scratch_shapes=[pltpu.VMEM((A_HEADS, BLK, 2 * BLK), F32),
                        pltpu.VMEM((per * A_HEADS // 2, BLK, 4 * BLK), BF16),
                        pltpu.VMEM((per * A_HEADS // 2, BLK, LANES), F32)],
        compiler_params=pltpu.CompilerParams(
            dimension_semantics=("arbitrary", "arbitrary"), vmem_limit_bytes=VMEM_LIMIT),
        name="swa",
    )(sinks, table, bucket, qa, kx, kx, vx, vx)


def _mla_kernel(q_ref, qn_ref, k_ref, v_ref, o_ref,
                s0_ref, s1_ref, s2_ref, x0_ref, x1_ref, x2_ref, m_ref, acc_ref):
    qi = pl.program_id(2)
    n_q = pl.num_programs(2)
    tq, tk, ts = MLA_TQ, MLA_TK, MLA_TS
    streams = tq // ts
    buf0 = (s0_ref, x0_ref)
    buf1 = (s1_ref, x1_ref)
    buf2 = (s2_ref, x2_ref)

    def scores(c, dst, queries=q_ref):
        s_ref, x_ref = dst
        start = pl.multiple_of(c * tk, tk)
        k = k_ref[pl.ds(start, tk), :]
        for h in range(streams):
            s = lax.dot_general(queries[h * ts:(h + 1) * ts, :], k,
                                (((1,), (1,)), ((), ())), preferred_element_type=F32)
            s_ref[h] = s
            blk = s[:, 0:LANES]
            for j in range(1, tk // LANES):
                blk = jnp.maximum(blk, s[:, j * LANES:(j + 1) * LANES])
            x_ref[h] = blk

    def accumulate(c, src, diagonal):
        s_ref, x_ref = src
        start = pl.multiple_of(c * tk, tk)
        for h in range(streams):
            nk = (h + 1) * ts if diagonal else tk
            v = v_ref[pl.ds(start, nk), :]
            v1 = jnp.concatenate([v, jnp.ones_like(v)], axis=1)
            s = s_ref[h, :, 0:nk]
            if diagonal:
                qrow = h * ts + lax.broadcasted_iota(jnp.int32, (ts, nk), 0)
                kcol = lax.broadcasted_iota(jnp.int32, (ts, nk), 1)
                s = jnp.where(qrow >= kcol, s, NEG)
                row_max = jnp.max(s, axis=-1, keepdims=True)
            else:
                row_max = jnp.max(x_ref[h], axis=-1, keepdims=True)
            m = m_ref[h]
            m_new = jnp.maximum(m, row_max)
            alpha = jnp.exp2(m - m_new)
            p = jnp.exp2(s - m_new).astype(BF16)
            acc_ref[h] = alpha * acc_ref[h] + jnp.dot(p, v1, preferred_element_type=F32)
            m_ref[h] = m_new

    def tick(c, src, dst):
        scores(c + 1, dst)
        accumulate(c, src, False)

    m_ref[...] = jnp.full(m_ref.shape, NEG, F32)
    acc_ref[...] = jnp.zeros(acc_ref.shape, F32)

    odd = (qi % 2) == 1
    even_later = (qi > 0) & jnp.logical_not(odd)

    @pl.when(qi == 0)
    def _():
        scores(0, buf0)

    @pl.when(odd)
    def _():
        tick(0, buf2, buf0)

    @pl.when(even_later)
    def _():
        tick(0, buf2, buf1)
        tick(1, buf1, buf0)

    def ticks(c, count):
        for i in range(0, count, 2):
            tick(c + i, buf0, buf1)
            tick(c + i + 1, buf1, buf0)

    done = jnp.where(qi == 0, 0, jnp.where(odd, 1, 2))
    rest = qi - done
    size = 2
    while size < MLA_UNROLL:
        pl.when((rest & size) != 0)(functools.partial(ticks, done, size))
        done = done + (rest & size)
        size *= 2

    def group(t, carry):
        ticks(done + MLA_UNROLL * t, MLA_UNROLL)
        return carry

    lax.fori_loop(0, rest // MLA_UNROLL, group, 0)

    @pl.when(qi + 1 < n_q)
    def _():
        scores(0, buf2, qn_ref)
        accumulate(qi, buf0, True)

    @pl.when(qi + 1 == n_q)
    def _():
        accumulate(qi, buf0, True)

    for h in range(streams):
        acc = acc_ref[h]
        o_ref[h * ts:(h + 1) * ts, :] = acc[:, :V_DIM] / acc[:, V_DIM:]


def _mla_call(qm, km, vb):
    b, s, _ = qm.shape
    tq, tk, ts = MLA_TQ, MLA_TK, MLA_TS
    assert tq == tk and tq % ts == 0
    streams = tq // ts
    n_q = s // tq
    score_buf = pltpu.VMEM((streams, ts, tk), F32)
    max_buf = pltpu.VMEM((streams, ts, LANES), F32)
    return pl.pallas_call(
        _mla_kernel,
        out_shape=jax.ShapeDtypeStruct((b, s, B_WIDTH), F32),
        grid=(b, B_HEADS, n_q),
        in_specs=[pl.BlockSpec((None, tq, QK_PAD), lambda bi, h, i: (bi, i, h)),
                  pl.BlockSpec((None, tq, QK_PAD),
                               lambda bi, h, i: (bi, jnp.minimum(i + 1, n_q - 1), h)),
                  pl.BlockSpec((None, s, QK_PAD), lambda bi, h, i: (bi, 0, h)),
                  pl.BlockSpec((None, s, V_DIM), lambda bi, h, i: (bi, 0, h))],
        out_specs=pl.BlockSpec((None, tq, V_DIM), lambda bi, h, i: (bi, i, h)),
        scratch_shapes=[score_buf, score_buf, score_buf, max_buf, max_buf, max_buf,
                        pltpu.VMEM((streams, ts, 1), F32),
                        pltpu.VMEM((streams, ts, 2 * V_DIM), F32)],
        compiler_params=pltpu.CompilerParams(
            dimension_semantics=("arbitrary", "arbitrary", "arbitrary"),
            vmem_limit_bytes=VMEM_LIMIT),
        name="mla",
    )(qm, qm, km, vb)


def _ffn_kernel(tiles_per_seq, x_ref, oa_ref, ob_ref, ag_ref, bg_ref, wout_ref, fg_ref,
                wup_ref, cw_ref, cb_ref, wdn_ref, ng_ref, o_ref,
                carry_ref, act0_ref, act1_ref, xa_ref, xb_ref):
    tm, tf = FFN_TM, FFN_TF
    step = pl.program_id(0)
    n_chunks = D_FF // tf
    n_slabs = D_MODEL // FFN_SLAB
    slab_after = {int((q + 0.5) * n_chunks / n_slabs): (q,) for q in range(n_slabs)}
    assert len(slab_after) == n_slabs

    @pl.when(step == 0)
    def _():
        act1_ref[...] = jnp.zeros_like(act1_ref)
        xb_ref[...] = jnp.zeros_like(xb_ref)

    @pl.when((step % tiles_per_seq) == 0)
    def _():
        carry_ref[...] = jnp.zeros_like(carry_ref)

    row8 = lax.broadcasted_iota(jnp.int32, (SUBLANES, tf), 0)

    def conv(u, cols):
        prev = carry_ref[:, cols]
        carry_ref[:, cols] = u[tm - SUBLANES:, :]
        last, before_last = prev[SUBLANES - 1:SUBLANES, :], prev[SUBLANES - 2:SUBLANES - 1, :]
        r1 = pltpu.roll(u, 1, 0)
        r2 = pltpu.roll(u, 2, 0)
        head1 = jnp.where(row8 == 0, last, r1[:SUBLANES])
        head2 = jnp.where(row8 == 0, before_last, jnp.where(row8 == 1, last, r2[:SUBLANES]))
        s1 = jnp.concatenate([head1, r1[SUBLANES:]], axis=0)
        s2 = jnp.concatenate([head2, r2[SUBLANES:]], axis=0)
        w = cw_ref[:, cols]
        y = s2 * w[0:1] + s1 * w[1:2] + u * w[2:3]
        return y + cb_ref[:, cols]

    def body(act_w, x_w, act_r, x_r):
        mixed = jnp.concatenate([_rms(oa_ref[...]) * ag_ref[...],
                                 _rms(ob_ref[...]) * bg_ref[...]], axis=-1)
        x1 = x_ref[...] + jnp.dot(mixed.astype(BF16), wout_ref[...],
                                  preferred_element_type=F32)
        x_w[...] = x1
        h2 = (_rms(x1) * fg_ref[...]).astype(BF16)

        for j in range(n_chunks):
            gcols = slice(j * tf, (j + 1) * tf)
            vcols = slice(D_FF + j * tf, D_FF + (j + 1) * tf)
            ug = jnp.dot(h2, wup_ref[:, gcols], preferred_element_type=F32)
            uv = jnp.dot(h2, wup_ref[:, vcols], preferred_element_type=F32)
            gate = conv(ug, gcols)
            val = conv(uv, vcols)
            act_w[:, gcols] = (gate * (1.0 / (1.0 + jnp.exp(-gate))) * val).astype(BF16)
            for q in slab_after.get(j, ()):
                cols = slice(q * FFN_SLAB, (q + 1) * FFN_SLAB)
                o_ref[:, cols] = x_r[:, cols] + jnp.dot(act_r[...], wdn_ref[:, cols],
                                                        preferred_element_type=F32)
        o_ref[...] = _rms(o_ref[...]) * ng_ref[...]

    pl.when(step % 2 == 0)(functools.partial(body, act0_ref, xa_ref, act1_ref, xb_ref))
    pl.when(step % 2 == 1)(functools.partial(body, act1_ref, xb_ref, act0_ref, xa_ref))


def _ffn_call(x2, oa, ob, ag, bg, wout, fg, wup, cw, cb, wdn, ng, tiles_per_seq):
    t = x2.shape[0]
    tm = FFN_TM
    n_tiles = t // tm
    row = lambda i: (jnp.minimum(i, n_tiles - 1), 0)
    out_row = lambda i: (jnp.maximum(i - 1, 0), 0)
    fixed = lambda i: (0, 0)
    full = lambda a: pl.BlockSpec(a.shape, fixed, pipeline_mode=pl.Buffered(1))
    return pl.pallas_call(
        functools.partial(_ffn_kernel, tiles_per_seq),
        out_shape=jax.ShapeDtypeStruct((t, D_MODEL), F32),
        grid=(n_tiles + 1,),
        in_specs=[pl.BlockSpec((tm, D_MODEL), row), pl.BlockSpec((tm, A_WIDTH), row),
                  pl.BlockSpec((tm, B_WIDTH), row), full(ag), full(bg), full(wout), full(fg),
                  full(wup), full(cw), full(cb), full(wdn), full(ng)],
        out_specs=pl.BlockSpec((tm, D_MODEL), out_row),
        scratch_shapes=[pltpu.VMEM((SUBLANES, 2 * D_FF), F32),
                        pltpu.VMEM((tm, D_FF), BF16),
                        pltpu.VMEM((tm, D_FF), BF16),
                        pltpu.VMEM((tm, D_MODEL), F32),
                        pltpu.VMEM((tm, D_MODEL), F32)],
        compiler_params=pltpu.CompilerParams(
            dimension_semantics=("arbitrary",), vmem_limit_bytes=VMEM_LIMIT),
        name="ffn",
    )(x2, oa, ob, ag, bg, wout, fg, wup, cw, cb, wdn, ng)


def _t5_bucket_matrix():
    q_idx = BLK + np.arange(BLK)
    k_idx = np.arange(2 * BLK)
    dist = q_idx[:, None] - k_idx[None, :]
    max_exact = NUM_BUCKETS // 2
    n = np.maximum(dist, 0)
    large = max_exact + (np.log(np.maximum(n, 1).astype(np.float32) / max_exact)
                         / math.log(T5_MAX_DIST / max_exact)
                         * (NUM_BUCKETS - max_exact)).astype(np.int32)
    large = np.minimum(large, NUM_BUCKETS - 1)
    bucket = np.where(n < max_exact, n, large)
    in_window = (dist >= 0) & (dist < WINDOW)
    return np.where(in_window, bucket, -1).astype(np.int32)


def _row(v):
    return v.reshape(1, -1).astype(F32)


def kernel(x, positions, rel_bias_table, attn_norm_g, w_in, sinks, q_norm_g, w_q_b, kv_norm_g,
           w_kv_b, a_out_norm_g, b_out_norm_g, w_out, ffn_norm_g, w_up, conv_w, conv_b, w_down,
           final_norm_g):
    b, s, d = x.shape
    t = b * s
    assert d == D_MODEL and s % max(PROJ_TM, SWA_ROWS, MLA_TQ, FFN_TM) == 0
    assert attn_norm_g.shape[0] == 1, "one trunk layer"
    l = 0

    zeros64 = jnp.zeros((D_MODEL, LANES - ROPE_DIM), F32)
    kp = w_in[l][:, _C_KPA:_C_KPA + ROPE_DIM]
    k1, k2 = kp[:, :ROPE_HALF], kp[:, ROPE_HALF:]
    w_in_x = jnp.concatenate([w_in[l][:, :_C_KPA], k1, k2, zeros64], axis=1).astype(BF16)
    wq = w_q_b[l].reshape(Q_LORA, B_HEADS, NOPE_DIM + ROPE_DIM)
    q1 = wq[:, :, NOPE_DIM:NOPE_DIM + ROPE_HALF]
    q2 = wq[:, :, NOPE_DIM + ROPE_HALF:]
    zq = jnp.zeros((Q_LORA, B_HEADS, LANES - ROPE_DIM), F32)
    wq_x = jnp.concatenate([wq[:, :, :NOPE_DIM], q1, q2, zq], axis=2)
    wq_x = wq_x.reshape(Q_LORA, B_HEADS * _QB_HEAD).astype(BF16)
    wkv = w_kv_b[l].reshape(KV_LORA, B_HEADS, NOPE_DIM + V_DIM)
    wkv_x = jnp.concatenate([wkv[:, :, :NOPE_DIM].reshape(KV_LORA, -1),
                             wkv[:, :, NOPE_DIM:].reshape(KV_LORA, -1)], axis=1).astype(BF16)

    inv_freq = ROPE_THETA ** (-jnp.arange(0, ROPE_DIM, 2, dtype=F32) / ROPE_DIM)
    freq = jnp.tile(inv_freq, LANES // ROPE_HALF).reshape(1, LANES)

    x2 = x.reshape(t, d)
    pos2 = positions.reshape(t, 1).astype(jnp.int32)

    qa, ka, va, qm, km, vb = _proj_call(
        x2, pos2, _row(attn_norm_g[l]), w_in_x, _row(q_norm_g[l]), wq_x,
        _row(kv_norm_g[l]), wkv_x, freq)

    bucket = jnp.asarray(_t5_bucket_matrix())
    out_a = _swa_call(sinks[l].astype(F32), rel_bias_table.astype(F32), bucket,
                      qa.reshape(b, s, -1), ka.reshape(b, s, -1), va.reshape(b, s, -1))
    out_b = _mla_call(qm.reshape(b, s, -1), km.reshape(b, s, -1), vb.reshape(b, s, -1))

    cw = jnp.concatenate([conv_w[l], jnp.zeros((SUBLANES - CONV_W, 2 * D_FF), F32)], axis=0)
    out = _ffn_call(
        x2, out_a.reshape(t, -1), out_b.reshape(t, -1), _row(a_out_norm_g[l]),
        _row(b_out_norm_g[l]), w_out[l].astype(BF16), _row(ffn_norm_g[l]),
        w_up[l].astype(BF16), cw.astype(F32), _row(conv_b[l]), w_down[l].astype(BF16),
        _row(final_norm_g), s // FFN_TM)
    return out.reshape(b, s, d)
```

```python
import functools
import math

import numpy as np
import jax
import jax.numpy as jnp
from jax import lax
from jax.experimental import pallas as pl
from jax.experimental.pallas import tpu as pltpu

F32 = jnp.float32
BF16 = jnp.bfloat16

D_MODEL = 1024
A_HEADS = 8
A_KV_HEADS = 2
A_HEAD_DIM = 64
A_GROUP = A_HEADS // A_KV_HEADS
WINDOW = 128
BLK = WINDOW
A_WIDTH = A_HEADS * A_HEAD_DIM
A_KV_WIDTH = A_KV_HEADS * A_HEAD_DIM
NUM_BUCKETS = 32
T5_MAX_DIST = 128
B_HEADS = 4
Q_LORA = 256
KV_LORA = 128
NOPE_DIM = 128
ROPE_DIM = 64
ROPE_HALF = ROPE_DIM // 2
V_DIM = 128
ROPE_THETA = 10000.0
B_WIDTH = B_HEADS * V_DIM
D_FF = 2816
CONV_W = 3
EPS = 1e-6
NEG = -1e30
LOG2E = math.log2(math.e)

LANES = 128
SUBLANES = 8
QK_PAD = 256
VMEM_LIMIT = 56 * 1024 * 1024

PROJ_TM = 1024
PROJ_TS = 512
SWA_ROWS = 1024
MLA_TQ = 1024
MLA_TK = 1024
MLA_TS = 256
MLA_UNROLL = 8
FFN_TM = 512
FFN_TF = 256
FFN_SLAB = 256
FFN_U_BUFS = 4
FFN_ROW_SPLIT = 2

_C_QA = 0
_C_KA = _C_QA + A_WIDTH
_C_VA = _C_KA + A_KV_WIDTH
_C_CQ = _C_VA + A_KV_WIDTH
_C_CKV = _C_CQ + Q_LORA
_C_KPA = _C_CKV + KV_LORA
_C_END = _C_KPA + LANES
_QB_HEAD = 2 * LANES


def _rms(x):
    return x * lax.rsqrt(jnp.mean(x * x, axis=-1, keepdims=True) + EPS)


def _proj_kernel(x_ref, pos_ref, g_ref, w_in_ref, qg_ref, wq_ref, kvg_ref, wkv_ref,
                 freq_ref, qa_ref, ka_ref, va_ref, qm_ref, km_ref, vb_ref,
                 cos_ref, sin_ref):
    quarter = PROJ_TM // 4
    lane = lax.broadcasted_iota(jnp.int32, (quarter, LANES), 1)
    pos_q = [pos_ref[j * quarter:(j + 1) * quarter, :].astype(F32) for j in range(4)]
    pos_d = jnp.where(lane < ROPE_HALF, pos_q[0],
                      jnp.where(lane < 2 * ROPE_HALF, pos_q[1],
                                jnp.where(lane < 3 * ROPE_HALF, pos_q[2], pos_q[3])))
    ang = pos_d * freq_ref[...]
    for table, dst_ref, sign in ((jnp.cos(ang), cos_ref, 1.0), (jnp.sin(ang), sin_ref, -1.0)):
        for j in range(4):
            at0 = pltpu.roll(table, (LANES - j * ROPE_HALF) % LANES, 1)
            at1 = pltpu.roll(table, (LANES + ROPE_HALF - j * ROPE_HALF) % LANES, 1)
            spread = jnp.where(lane < ROPE_HALF, sign * at0, at1)
            if dst_ref is sin_ref:
                spread = jnp.where(lane < ROPE_DIM, spread, 0.0)
            dst_ref[j * quarter:(j + 1) * quarter, :] = spread

    lane_s = lax.broadcasted_iota(jnp.int32, (PROJ_TS, LANES), 1)

    def swap(x):
        return jnp.where(lane_s < ROPE_HALF, pltpu.roll(x, LANES - ROPE_HALF, 1),
                         pltpu.roll(x, ROPE_HALF, 1))

    n_st = PROJ_TM // PROJ_TS
    rows = [slice(st * PROJ_TS, (st + 1) * PROJ_TS) for st in range(n_st)]
    half = LANES // 2
    projs = []
    for r in rows:
        h = _rms(x_ref[r, :]) * g_ref[...]
        projs.append(jnp.dot(h.astype(BF16), w_in_ref[...], preferred_element_type=F32))

    ups = []
    for r, proj in zip(rows, projs):
        qa_ref[r, :] = (proj[:, _C_QA:_C_KA] * (A_HEAD_DIM ** -0.5 * LOG2E)).astype(BF16)
        for dst_ref, c0 in ((ka_ref, _C_KA), (va_ref, _C_VA)):
            nat = proj[:, c0:c0 + A_KV_WIDTH]
            dst_ref[r, :A_KV_WIDTH] = nat.astype(BF16)
            dst_ref[r, A_KV_WIDTH:] = pltpu.roll(nat, half, 1).astype(BF16)
        qn = _rms(proj[:, _C_CQ:_C_CKV]) * qg_ref[...]
        qb = jnp.dot(qn.astype(BF16), wq_ref[...], preferred_element_type=F32)
        kvn = _rms(proj[:, _C_CKV:_C_KPA]) * kvg_ref[...]
        kv = jnp.dot(kvn.astype(BF16), wkv_ref[...], preferred_element_type=F32)
        ups.append((qb, kv))

    scale = (NOPE_DIM + ROPE_DIM) ** -0.5 * LOG2E
    for r, proj, (qb, kv) in zip(rows, projs, ups):
        cc = cos_ref[r, :]
        ss = sin_ref[r, :]
        k_rot = proj[:, _C_KPA:_C_END]
        kpe = (k_rot * cc + swap(k_rot) * ss).astype(BF16)
        for hd in range(B_HEADS):
            c0 = hd * _QB_HEAD
            nope = qb[:, c0:c0 + LANES]
            q_rot = qb[:, c0 + LANES:c0 + 2 * LANES]
            o0 = hd * QK_PAD
            qm_ref[r, o0:o0 + LANES] = (nope * scale).astype(BF16)
            qm_ref[r, o0 + LANES:o0 + 2 * LANES] = (
                (q_rot * cc + swap(q_rot) * ss) * scale).astype(BF16)
            km_ref[r, o0:o0 + LANES] = kv[:, hd * NOPE_DIM:(hd + 1) * NOPE_DIM].astype(BF16)
            km_ref[r, o0 + LANES:o0 + 2 * LANES] = kpe
        vb_ref[r, :] = kv[:, B_HEADS * NOPE_DIM:].astype(BF16)


def _proj_call(x2, pos2, g, w_in, qg, wq, kvg, wkv, freq):
    t = x2.shape[0]
    tm = PROJ_TM
    row = lambda i: (i, 0)
    fixed = lambda i: (0, 0)
    full = lambda a: pl.BlockSpec(a.shape, fixed)
    out_shape = (
        jax.ShapeDtypeStruct((t, A_WIDTH), BF16),
        jax.ShapeDtypeStruct((t, 2 * A_KV_WIDTH), BF16),
        jax.ShapeDtypeStruct((t, 2 * A_KV_WIDTH), BF16),
        jax.ShapeDtypeStruct((t, B_HEADS * QK_PAD), BF16),
        jax.ShapeDtypeStruct((t, B_HEADS * QK_PAD), BF16),
        jax.ShapeDtypeStruct((t, B_WIDTH), BF16),
    )
    return pl.pallas_call(
        _proj_kernel,
        out_shape=out_shape,
        grid=(t // tm,),
        in_specs=[pl.BlockSpec((tm, D_MODEL), row), pl.BlockSpec((tm, 1), row),
                  full(g), full(w_in), full(qg), full(wq), full(kvg), full(wkv),
                  full(freq)],
        out_specs=tuple(pl.BlockSpec((tm, s.shape[1]), row) for s in out_shape),
        scratch_shapes=[pltpu.VMEM((tm, LANES), F32), pltpu.VMEM((tm, LANES), F32)],
        compiler_params=pltpu.CompilerParams(
            dimension_semantics=("arbitrary",), vmem_limit_bytes=VMEM_LIMIT),
        name="proj",
    )(x2, pos2, g, w_in, qg, wq, kvg, wkv, freq)


def _swa_kernel(sinks_ref, tbl_ref, bucket_ref, q_ref, kp_ref, kc_ref, vp_ref, vc_ref,
                o_ref, bias_ref, p_ref, t_ref):
    step = pl.program_id(1)
    half = LANES // 2

    @pl.when((pl.program_id(0) == 0) & (step == 0))
    def _():
        bk = bucket_ref[...]
        for hd in range(A_HEADS):
            acc = jnp.full((BLK, 2 * BLK), NEG, F32)
            for b in range(NUM_BUCKETS):
                acc = jnp.where(bk == b, tbl_ref[b, hd] * LOG2E, acc)
            bias_ref[hd] = acc

    kidx = lax.broadcasted_iota(jnp.int32, (BLK, 2 * BLK), 1)
    pad_mask = kidx < jnp.where(step == 0, BLK, 0)

    def lane_mask(n_rows, low):
        lane = lax.broadcasted_iota(jnp.int32, (n_rows, LANES), 1)
        return jnp.where((lane < half) == low, 1.0, 0.0).astype(BF16)

    n_keys = BLK + SWA_ROWS
    lo, hi = lane_mask(n_keys, True), lane_mask(n_keys, False)
    ones_low, ones_high = lane_mask(2 * BLK, True), lane_mask(2 * BLK, False)
    lo_q = lax.broadcasted_iota(jnp.int32, (BLK, LANES), 1) < half

    k_all = jnp.concatenate([kp_ref[...], kc_ref[...]], axis=0)
    v_all = jnp.concatenate([vp_ref[...], vc_ref[...]], axis=0)
    k_nat, k_swp = k_all[:, :LANES], k_all[:, LANES:]
    v_nat, v_swp = v_all[:, :LANES], v_all[:, LANES:]
    k_low, k_high = (k_nat * lo, k_swp * lo), (k_swp * hi, k_nat * hi)
    v_low, v_high = (v_nat * lo, v_swp * lo), (v_swp * hi, v_nat * hi)

    dims = (((1,), (1,)), ((), ()))
    n_blk = SWA_ROWS // BLK
    pairs = A_GROUP // 2

    def pass1(r):
        keys = slice(r * BLK, (r + 2) * BLK)
        q_blk = q_ref[r * BLK:(r + 1) * BLK, :]
        for g in range(A_KV_HEADS):
            qg = jnp.concatenate([q_blk[:, (2 * g) * LANES:(2 * g + 1) * LANES],
                                  q_blk[:, (2 * g + 1) * LANES:(2 * g + 2) * LANES]], axis=0)
            s_pair = [lax.dot_general(qg, k_side[g][keys], dims, preferred_element_type=F32)
                      for k_side in (k_low, k_high)]
            for j in range(pairs):
                pair = (r * A_KV_HEADS + g) * pairs + j
                sink_terms = []
                for e in range(2):
                    hd = A_GROUP * g + 2 * j + e
                    s = s_pair[e][j * BLK:(j + 1) * BLK, :]
                    s = s + bias_ref[hd]
                    if r == 0:
                        s = jnp.where(pad_mask, NEG, s)
                    sink = sinks_ref[hd] * LOG2E
                    m = jnp.maximum(jnp.max(s, axis=-1, keepdims=True), sink)
                    p_ref[pair, :, e * 2 * BLK:(e + 1) * 2 * BLK] = jnp.exp2(s - m).astype(BF16)
                    sink_terms.append(jnp.exp2(sink - m))
                t_ref[pair] = jnp.where(lo_q, sink_terms[0], sink_terms[1])

    def pass2(r):
        rows = slice(r * BLK, (r + 1) * BLK)
        keys = slice(r * BLK, (r + 2) * BLK)
        for g in range(A_KV_HEADS):
            rhs = jnp.concatenate(
                [jnp.concatenate([v_low[g][keys], ones_low], axis=1),
                 jnp.concatenate([v_high[g][keys], ones_high], axis=1)], axis=0)
            for j in range(pairs):
                pair = (r * A_KV_HEADS + g) * pairs + j
                res = jnp.dot(p_ref[pair], rhs, preferred_element_type=F32)
                col = (g * pairs + j) * LANES
                o_ref[rows, col:col + LANES] = res[:, :LANES] / (res[:, LANES:] + t_ref[pair])

    for r in range(n_blk):
        pass1(r)
    for r in range(n_blk):
        pass2(r)


def _swa_call(sinks, table, bucket, qa, kx, vx):
    b, s, _ = qa.shape
    rows = SWA_ROWS
    per = rows // BLK
    cur = lambda bi, i: (bi, i, 0)
    prev = lambda bi, i: (bi, jnp.maximum(i * per - 1, 0), 0)
    smem = pl.BlockSpec(memory_space=pltpu.SMEM)
    return pl.pallas_call(
        _swa_kernel,
        out_shape=jax.ShapeDtypeStruct((b, s, A_WIDTH), F32),
        grid=(b, s // rows),
        in_specs=[smem, smem, pl.BlockSpec(bucket.shape, lambda bi, i: (0, 0)),
                  pl.BlockSpec((None, rows, A_WIDTH), cur),
                  pl.BlockSpec((None, BLK, 2 * A_KV_WIDTH), prev),
                  pl.BlockSpec((None, rows, 2 * A_KV_WIDTH), cur),
                  pl.BlockSpec((None, BLK, 2 * A_KV_WIDTH), prev),
                  pl.BlockSpec((None, rows, 2 * A_KV_WIDTH), cur)],
        out_specs=pl.BlockSpec((None, rows, A_WIDTH), cur),
        scratch_shapes=[pltpu.VMEM((A_HEADS, BLK, 2 * BLK), F32),
                        pltpu.VMEM((per * A_HEADS // 2, BLK, 4 * BLK), BF16),
                        pltpu.VMEM((per * A_HEADS // 2, BLK, LANES), F32)],
        compiler_params=pltpu.CompilerParams(
            dimension_semantics=("arbitrary", "arbitrary"), vmem_limit_bytes=VMEM_LIMIT),
        name="swa",
    )(sinks, table, bucket, qa, kx, kx, vx, vx)


def _mla_kernel(q_ref, qn_ref, k_ref, v_ref, o_ref,
                s0_ref, s1_ref, s2_ref, x0_ref, x1_ref, x2_ref, m_ref, acc_ref):
    qi = pl.program_id(2)
    n_q = pl.num_programs(2)
    tq, tk, ts = MLA_TQ, MLA_TK, MLA_TS
    streams = tq // ts
    buf0 = (s0_ref, x0_ref)
    buf1 = (s1_ref, x1_ref)
    buf2 = (s2_ref, x2_ref)

    def scores(c, dst, queries=q_ref):
        s_ref, x_ref = dst
        start = pl.multiple_of(c * tk, tk)
        k = k_ref[pl.ds(start, tk), :]
        for h in range(streams):
            s = lax.dot_general(queries[h * ts:(h + 1) * ts, :], k,
                                (((1,), (1,)), ((), ())), preferred_element_type=F32)
            s_ref[h] = s
            blk = s[:, 0:LANES]
            for j in range(1, tk // LANES):
                blk = jnp.maximum(blk, s[:, j * LANES:(j + 1) * LANES])
            x_ref[h] = blk

    def accumulate(c, src, diagonal):
        s_ref, x_ref = src
        start = pl.multiple_of(c * tk, tk)
        for h in range(streams):
            nk = (h + 1) * ts if diagonal else tk
            v = v_ref[pl.ds(start, nk), :]
            v1 = jnp.concatenate([v, jnp.ones_like(v)], axis=1)
            s = s_ref[h, :, 0:nk]
            if diagonal:
                qrow = h * ts + lax.broadcasted_iota(jnp.int32, (ts, nk), 0)
                kcol = lax.broadcasted_iota(jnp.int32, (ts, nk), 1)
                s = jnp.where(qrow >= kcol, s, NEG)
                row_max = jnp.max(s, axis=-1, keepdims=True)
            else:
                row_max = jnp.max(x_ref[h], axis=-1, keepdims=True)
            m = m_ref[h]
            m_new = jnp.maximum(m, row_max)
            alpha = jnp.exp2(m - m_new)
            p = jnp.exp2(s - m_new).astype(BF16)
            acc_ref[h] = alpha * acc_ref[h] + jnp.dot(p, v1, preferred_element_type=F32)
            m_ref[h] = m_new

    def tick(c, src, dst):
        scores(c + 1, dst)
        accumulate(c, src, False)

    m_ref[...] = jnp.full(m_ref.shape, NEG, F32)
    acc_ref[...] = jnp.zeros(acc_ref.shape, F32)

    odd = (qi % 2) == 1
    even_later = (qi > 0) & jnp.logical_not(odd)

    @pl.when(qi == 0)
    def _():
        scores(0, buf0)

    @pl.when(odd)
    def _():
        tick(0, buf2, buf0)

    @pl.when(even_later)
    def _():
        tick(0, buf2, buf1)
        tick(1, buf1, buf0)

    def ticks(c, count):
        for i in range(0, count, 2):
            tick(c + i, buf0, buf1)
            tick(c + i + 1, buf1, buf0)

    done = jnp.where(qi == 0, 0, jnp.where(odd, 1, 2))
    rest = qi - done
    size = 2
    while size < MLA_UNROLL:
        pl.when((rest & size) != 0)(functools.partial(ticks, done, size))
        done = done + (rest & size)
        size *= 2

    def group(t, carry):
        ticks(done + MLA_UNROLL * t, MLA_UNROLL)
        return carry

    lax.fori_loop(0, rest // MLA_UNROLL, group, 0)

    @pl.when(qi + 1 < n_q)
    def _():
        scores(0, buf2, qn_ref)
        accumulate(qi, buf0, True)

    @pl.when(qi + 1 == n_q)
    def _():
        accumulate(qi, buf0, True)

    for h in range(streams):
        acc = acc_ref[h]
        o_ref[h * ts:(h + 1) * ts, :] = acc[:, :V_DIM] / acc[:, V_DIM:]


def _mla_call(qm, km, vb):
    b, s, _ = qm.shape
    tq, tk, ts = MLA_TQ, MLA_TK, MLA_TS
    assert tq == tk and tq % ts == 0
    streams = tq // ts
    n_q = s // tq
    score_buf = pltpu.VMEM((streams, ts, tk), F32)
    max_buf = pltpu.VMEM((streams, ts, LANES), F32)
    return pl.pallas_call(
        _mla_kernel,
        out_shape=jax.ShapeDtypeStruct((b, s, B_WIDTH), F32),
        grid=(b, B_HEADS, n_q),
        in_specs=[pl.BlockSpec((None, tq, QK_PAD), lambda bi, h, i: (bi, i, h)),
                  pl.BlockSpec((None, tq, QK_PAD),
                               lambda bi, h, i: (bi, jnp.minimum(i + 1, n_q - 1), h)),
                  pl.BlockSpec((None, s, QK_PAD), lambda bi, h, i: (bi, 0, h)),
                  pl.BlockSpec((None, s, V_DIM), lambda bi, h, i: (bi, 0, h))],
        out_specs=pl.BlockSpec((None, tq, V_DIM), lambda bi, h, i: (bi, i, h)),
        scratch_shapes=[score_buf, score_buf, score_buf, max_buf, max_buf, max_buf,
                        pltpu.VMEM((streams, ts, 1), F32),
                        pltpu.VMEM((streams, ts, 2 * V_DIM), F32)],
        compiler_params=pltpu.CompilerParams(
            dimension_semantics=("arbitrary", "arbitrary", "arbitrary"),
            vmem_limit_bytes=VMEM_LIMIT),
        name="mla",
    )(qm, qm, km, vb)


def _ffn_kernel(tiles_per_seq, x_ref, oa_ref, ob_ref, ag_ref, bg_ref, wout_ref, fg_ref,
                wup_ref, cw_ref, cb_ref, wdn_ref, ng_ref, o_ref,
                carry_ref, u_ref, act0_ref, act1_ref, xa_ref, xb_ref):
    tm, tf = FFN_TM, FFN_TF
    step = pl.program_id(0)
    n_chunks = D_FF // tf
    n_slabs = D_MODEL // FFN_SLAB
    slab_after = {int((q + 0.5) * n_chunks / n_slabs): (q,) for q in range(n_slabs)}
    assert len(slab_after) == n_slabs

    @pl.when(step == 0)
    def _():
        act1_ref[...] = jnp.zeros_like(act1_ref)
        xb_ref[...] = jnp.zeros_like(xb_ref)

    @pl.when((step % tiles_per_seq) == 0)
    def _():
        carry_ref[...] = jnp.zeros_like(carry_ref)

    def stage(u_ref, u, cols):
        u_ref[0:SUBLANES, :] = carry_ref[:, cols]
        u_ref[SUBLANES:SUBLANES + tm, :] = u
        carry_ref[:, cols] = u_ref[tm:tm + SUBLANES, :]

    def taps(u_ref, r0, nr, cols):
        w = cw_ref[:, cols]
        y = (u_ref[SUBLANES - 2 + r0:SUBLANES - 2 + r0 + nr, :] * w[0:1]
             + u_ref[SUBLANES - 1 + r0:SUBLANES - 1 + r0 + nr, :] * w[1:2]
             + u_ref[SUBLANES + r0:SUBLANES + r0 + nr, :] * w[2:3])
        return y + cb_ref[:, cols]

    def body(act_w, x_w, act_r, x_r):
        mixed = jnp.concatenate([_rms(oa_ref[...]) * ag_ref[...],
                                 _rms(ob_ref[...]) * bg_ref[...]], axis=-1)
        x1 = x_ref[...] + jnp.dot(mixed.astype(BF16), wout_ref[...],
                                  preferred_element_type=F32)
        x_w[...] = x1
        h2 = (_rms(x1) * fg_ref[...]).astype(BF16)

        for j in range(n_chunks):
            gcols = slice(j * tf, (j + 1) * tf)
            vcols = slice(D_FF + j * tf, D_FF + (j + 1) * tf)
            ug = jnp.dot(h2, wup_ref[:, gcols], preferred_element_type=F32)
            uv = jnp.dot(h2, wup_ref[:, vcols], preferred_element_type=F32)
            g_ref = u_ref.at[(2 * j) % FFN_U_BUFS]
            v_ref = u_ref.at[(2 * j + 1) % FFN_U_BUFS]
            stage(g_ref, ug, gcols)
            stage(v_ref, uv, vcols)
            nr = tm // FFN_ROW_SPLIT
            for part in range(FFN_ROW_SPLIT):
                gate = taps(g_ref, part * nr, nr, gcols)
                val = taps(v_ref, part * nr, nr, vcols)
                act_w[part * nr:(part + 1) * nr, gcols] = (
                    gate * (1.0 / (1.0 + jnp.exp(-gate))) * val).astype(BF16)
            for q in slab_after.get(j, ()):
                cols = slice(q * FFN_SLAB, (q + 1) * FFN_SLAB)
                o_ref[:, cols] = x_r[:, cols] + jnp.dot(act_r[...], wdn_ref[:, cols],
                                                        preferred_element_type=F32)
        o_ref[...] = _rms(o_ref[...]) * ng_ref[...]

    pl.when(step % 2 == 0)(functools.partial(body, act0_ref, xa_ref, act1_ref, xb_ref))
    pl.when(step % 2 == 1)(functools.partial(body, act1_ref, xb_ref, act0_ref, xa_ref))


def _ffn_call(x2, oa, ob, ag, bg, wout, fg, wup, cw, cb, wdn, ng, tiles_per_seq):
    t = x2.shape[0]
    tm = FFN_TM
    n_tiles = t // tm
    row = lambda i: (jnp.minimum(i, n_tiles - 1), 0)
    out_row = lambda i: (jnp.maximum(i - 1, 0), 0)
    fixed = lambda i: (0, 0)
    full = lambda a: pl.BlockSpec(a.shape, fixed, pipeline_mode=pl.Buffered(1))
    return pl.pallas_call(
        functools.partial(_ffn_kernel, tiles_per_seq),
        out_shape=jax.ShapeDtypeStruct((t, D_MODEL), F32),
        grid=(n_tiles + 1,),
        in_specs=[pl.BlockSpec((tm, D_MODEL), row), pl.BlockSpec((tm, A_WIDTH), row),
                  pl.BlockSpec((tm, B_WIDTH), row), full(ag), full(bg), full(wout), full(fg),
                  full(wup), full(cw), full(cb), full(wdn), full(ng)],
        out_specs=pl.BlockSpec((tm, D_MODEL), out_row),
        scratch_shapes=[pltpu.VMEM((SUBLANES, 2 * D_FF), F32),
                        pltpu.VMEM((FFN_U_BUFS, tm + SUBLANES, FFN_TF), F32),
                        pltpu.VMEM((tm, D_FF), BF16),
                        pltpu.VMEM((tm, D_FF), BF16),
                        pltpu.VMEM((tm, D_MODEL), F32),
                        pltpu.VMEM((tm, D_MODEL), F32)],
        compiler_params=pltpu.CompilerParams(
            dimension_semantics=("arbitrary",), vmem_limit_bytes=VMEM_LIMIT),
        name="ffn",
    )(x2, oa, ob, ag, bg, wout, fg, wup, cw, cb, wdn, ng)


def _t5_bucket_matrix():
    q_idx = BLK + np.arange(BLK)
    k_idx = np.arange(2 * BLK)
    dist = q_idx[:, None] - k_idx[None, :]
    max_exact = NUM_BUCKETS // 2
    n = np.maximum(dist, 0)
    large = max_exact + (np.log(np.maximum(n, 1).astype(np.float32) / max_exact)
                         / math.log(T5_MAX_DIST / max_exact)
                         * (NUM_BUCKETS - max_exact)).astype(np.int32)
    large = np.minimum(large, NUM_BUCKETS - 1)
    bucket = np.where(n < max_exact, n, large)
    in_window = (dist >= 0) & (dist < WINDOW)
    return np.where(in_window, bucket, -1).astype(np.int32)


def _row(v):
    return v.reshape(1, -1).astype(F32)


def kernel(x, positions, rel_bias_table, attn_norm_g, w_in, sinks, q_norm_g, w_q_b, kv_norm_g,
           w_kv_b, a_out_norm_g, b_out_norm_g, w_out, ffn_norm_g, w_up, conv_w, conv_b, w_down,
           final_norm_g):
    b, s, d = x.shape
    t = b * s
    assert d == D_MODEL and s % max(PROJ_TM, SWA_ROWS, MLA_TQ, FFN_TM) == 0
    assert attn_norm_g.shape[0] == 1, "one trunk layer"
    l = 0

    zeros64 = jnp.zeros((D_MODEL, LANES - ROPE_DIM), F32)
    kp = w_in[l][:, _C_KPA:_C_KPA + ROPE_DIM]
    k1, k2 = kp[:, :ROPE_HALF], kp[:, ROPE_HALF:]
    w_in_x = jnp.concatenate([w_in[l][:, :_C_KPA], k1, k2, zeros64], axis=1).astype(BF16)
    wq = w_q_b[l].reshape(Q_LORA, B_HEADS, NOPE_DIM + ROPE_DIM)
    q1 = wq[:, :, NOPE_DIM:NOPE_DIM + ROPE_HALF]
    q2 = wq[:, :, NOPE_DIM + ROPE_HALF:]
    zq = jnp.zeros((Q_LORA, B_HEADS, LANES - ROPE_DIM), F32)
    wq_x = jnp.concatenate([wq[:, :, :NOPE_DIM], q1, q2, zq], axis=2)
    wq_x = wq_x.reshape(Q_LORA, B_HEADS * _QB_HEAD).astype(BF16)
    wkv = w_kv_b[l].reshape(KV_LORA, B_HEADS, NOPE_DIM + V_DIM)
    wkv_x = jnp.concatenate([wkv[:, :, :NOPE_DIM].reshape(KV_LORA, -1),
                             wkv[:, :, NOPE_DIM:].reshape(KV_LORA, -1)], axis=1).astype(BF16)

    inv_freq = ROPE_THETA ** (-jnp.arange(0, ROPE_DIM, 2, dtype=F32) / ROPE_DIM)
    freq = jnp.tile(inv_freq, LANES // ROPE_HALF).reshape(1, LANES)

    x2 = x.reshape(t, d)
    pos2 = positions.reshape(t, 1).astype(jnp.int32)

    qa, ka, va, qm, km, vb = _proj_call(
        x2, pos2, _row(attn_norm_g[l]), w_in_x, _row(q_norm_g[l]), wq_x,
        _row(kv_norm_g[l]), wkv_x, freq)

    bucket = jnp.asarray(_t5_bucket_matrix())
    out_a = _swa_call(sinks[l].astype(F32), rel_bias_table.astype(F32), bucket,
                      qa.reshape(b, s, -1), ka.reshape(b, s, -1), va.reshape(b, s, -1))
    out_b = _mla_call(qm.reshape(b, s, -1), km.reshape(b, s, -1), vb.reshape(b, s, -1))

    cw = jnp.concatenate([conv_w[l], jnp.zeros((SUBLANES - CONV_W, 2 * D_FF), F32)], axis=0)
    out = _ffn_call(
        x2, out_a.reshape(t, -1), out_b.reshape(t, -1), _row(a_out_norm_g[l]),
        _row(b_out_norm_g[l]), w_out[l].astype(BF16), _row(ffn_norm_g[l]),
        w_up[l].astype(BF16), cw.astype(F32), _row(conv_b[l]), w_down[l].astype(BF16),
        _row(final_norm_g), s // FFN_TM)
    return out.reshape(b, s, d)
```
